```python
import math
import jax, jax.numpy as jnp
from jax import lax
import numpy as np

D_MODEL = 1024
BATCH = 4
SEQ = 8192
DEPTH = 1
DEC_BATCH = 2
DEC_SEQ = 8192
PAST_LEN = 128

DA_HEADS = 4
DA_DK = 64
DA_DV = 2 * DA_DK
DA_WIDTH = DA_HEADS * DA_DV
ROT_DIM = DA_DK // 4
ROPE_THETA = 500000.0
Q_BLOCK = 128
GLA_HEADS = 4
GLA_DK = 64
GLA_DV = 128
GLA_QK_WIDTH = GLA_HEADS * GLA_DK
GLA_WIDTH = GLA_HEADS * GLA_DV
GLA_RANK = 16
GLA_TAU = 16.0
GLA_CHUNK = 64
D_FF = 4 * D_MODEL
ALPHA = (2 * DEPTH) ** 0.25
BETA = (8 * DEPTH) ** -0.25
LN_EPS = 1e-5
RMS_EPS = 1e-6
IN_SIZES = (
    DA_HEADS * 2 * DA_DK,
    DA_HEADS * 2 * DA_DK,
    DA_WIDTH,
    GLA_QK_WIDTH,
    GLA_QK_WIDTH,
    GLA_WIDTH,
    GLA_WIDTH,
    GLA_RANK,
    GLA_RANK,
    2 * D_MODEL,
)
D_IN = sum(IN_SIZES)

kernel_name = "hybrid_diffattn_gla_deepnorm_encoder"


def layer_norm(x, g, b):
    xf = x.astype(jnp.float32)
    mu = jnp.mean(xf, axis=-1, keepdims=True)
    var = jnp.mean(jnp.square(xf - mu), axis=-1, keepdims=True)
    y = (xf - mu) * lax.rsqrt(var + LN_EPS) * g.astype(jnp.float32) + b.astype(jnp.float32)
    return y.astype(x.dtype)


def rms_norm(x, g):
    xf = x.astype(jnp.float32)
    y = xf * lax.rsqrt(jnp.mean(jnp.square(xf), axis=-1, keepdims=True) + RMS_EPS)
    return y * g.astype(jnp.float32)


def rope_tables(S):
    inv = 1.0 / (ROPE_THETA ** (jnp.arange(0, ROT_DIM, 2, dtype=jnp.float32) / ROT_DIM))
    ang = jnp.arange(S, dtype=jnp.float32)[:, None] * inv[None, :]
    return jnp.cos(ang), jnp.sin(ang)


def apply_partial_rope(t, cos, sin):
    rot, rest = t[..., :ROT_DIM], t[..., ROT_DIM:]
    x1, x2 = rot[..., :ROT_DIM // 2], rot[..., ROT_DIM // 2:]
    c = cos[None, :, None, None, :].astype(t.dtype)
    s = sin[None, :, None, None, :].astype(t.dtype)
    rot = jnp.concatenate([x1 * c - x2 * s, x2 * c + x1 * s], axis=-1)
    return jnp.concatenate([rot, rest], axis=-1)


def diff_attention(q, k, v, lam, lam_init, subln_g):
    B, S = q.shape[0], q.shape[1]
    nb = S // Q_BLOCK
    qb = q.reshape(B, nb, Q_BLOCK, DA_HEADS, 2, DA_DK).transpose(1, 0, 2, 3, 4, 5)
    scale = DA_DK ** -0.5

    def block(qi):
        s = jnp.einsum('bqhmd,bkhmd->bhmqk', qi, k, preferred_element_type=jnp.float32) * scale
        p = jax.nn.softmax(s, axis=-1)
        a = p[:, :, 0] - lam * p[:, :, 1]
        return jnp.einsum('bhqk,bkhv->bqhv', a.astype(v.dtype), v)

    o = lax.map(block, qb)
    o = o.transpose(1, 0, 2, 3, 4).reshape(B, S, DA_HEADS, DA_DV)
    o = rms_norm(o, subln_g) * (1.0 - lam_init)
    return o.reshape(B, S, DA_WIDTH).astype(v.dtype)


def gla_direction(q, k, v, g):
    B, H, S, DK = q.shape
    DV = v.shape[-1]
    n = S // GLA_CHUNK

    def chunks(t):
        return t.reshape(B, H, n, GLA_CHUNK, t.shape[-1]).transpose(2, 0, 1, 3, 4)

    mask = jnp.tril(jnp.ones((GLA_CHUNK, GLA_CHUNK), dtype=bool))[:, :, None]

    def step(state, inp):
        qc, kc, vc, gc = inp
        b = jnp.cumsum(gc, axis=-2)
        inter = jnp.einsum('bhtk,bhkv->bhtv', qc * jnp.exp(b), state)
        diff = b[:, :, :, None, :] - b[:, :, None, :, :]
        decay = jnp.exp(jnp.where(mask, diff, -jnp.inf))
        attn = jnp.einsum('bhtk,bhsk,bhtsk->bhts', qc, kc, decay)
        intra = jnp.einsum('bhts,bhsv->bhtv', attn, vc)
        b_last = b[:, :, -1:, :]
        state = jnp.exp(b_last[:, :, 0, :])[..., None] * state + jnp.einsum(
            'bhsk,bhsv->bhkv', kc * jnp.exp(b_last - b), vc)
        return state, inter + intra

    state0 = jnp.zeros((B, H, DK, DV), jnp.float32)
    _, o = lax.scan(step, state0, (chunks(q), chunks(k), chunks(v), chunks(g)))
    return o.transpose(1, 2, 0, 3, 4).reshape(B, H, S, DV)


def gla_branch(q, k, v, r, z_f, z_b, w_dec_f, b_dec_f, w_dec_b, b_dec_b, norm_g):
    B, S = q.shape[0], q.shape[1]

    def heads(t, d):
        return t.reshape(B, S, GLA_HEADS, d).transpose(0, 2, 1, 3).astype(jnp.float32)

    g_f = jax.nn.log_sigmoid((z_f @ w_dec_f + b_dec_f).astype(jnp.float32)) / GLA_TAU
    g_b = jax.nn.log_sigmoid((z_b @ w_dec_b + b_dec_b).astype(jnp.float32)) / GLA_TAU
    qh = heads(q, GLA_DK) * (GLA_DK ** -0.5)
    kh = heads(k, GLA_DK)
    vh = heads(v, GLA_DV)
    flip = lambda t: jnp.flip(t, axis=2)
    o_fwd = gla_direction(qh, kh, vh, heads(g_f, GLA_DK))
    o_bwd = flip(gla_direction(flip(qh), flip(kh), flip(vh), flip(heads(g_b, GLA_DK))))
    o = (o_fwd + o_bwd).transpose(0, 2, 1, 3)
    o = rms_norm(o, norm_g) * jax.nn.silu(r.reshape(B, S, GLA_HEADS, GLA_DV).astype(jnp.float32))
    return o.reshape(B, S, GLA_WIDTH).astype(v.dtype)


def encoder_layer(x, l, w_in, lam_q1, lam_k1, lam_q2, lam_k2, subln_g, w_dec_f, b_dec_f,
                  w_dec_b, b_dec_b, gla_norm_g, w_br_a, w_br_b, b_gate, w_out,
                  ln1_g, ln1_b, w_mlp1, w_mlp2, ln2_g, ln2_b):
    B, S, _ = x.shape
    split_idx = np.cumsum(IN_SIZES)[:-1].tolist()
    proj = x @ w_in
    da_q, da_k, da_v, g_q, g_k, g_v, g_r, z_f, z_b, gates = jnp.split(proj, split_idx, axis=-1)

    cos, sin = rope_tables(S)
    q = apply_partial_rope(da_q.reshape(B, S, DA_HEADS, 2, DA_DK), cos, sin)
    k = apply_partial_rope(da_k.reshape(B, S, DA_HEADS, 2, DA_DK), cos, sin)
    v = da_v.reshape(B, S, DA_HEADS, DA_DV)
    lam_init = 0.8 - 0.6 * math.exp(-0.3 * l)
    lam = (jnp.exp(jnp.sum(lam_q1.astype(jnp.float32) * lam_k1.astype(jnp.float32)))
           - jnp.exp(jnp.sum(lam_q2.astype(jnp.float32) * lam_k2.astype(jnp.float32))) + lam_init)
    y_a = diff_attention(q, k, v, lam, lam_init, subln_g)

    y_b = gla_branch(g_q, g_k, g_v, g_r, z_f, z_b, w_dec_f, b_dec_f, w_dec_b, b_dec_b, gla_norm_g)

    gate_a, gate_b = jnp.split(jax.nn.sigmoid(gates + b_gate), 2, axis=-1)
    mix = (gate_a * (y_a @ w_br_a) + gate_b * (y_b @ w_br_b)) @ w_out
    x = layer_norm(ALPHA * x + mix, ln1_g, ln1_b)

    h = jnp.square(jax.nn.relu(x @ w_mlp1)) @ w_mlp2
    return layer_norm(ALPHA * x + h, ln2_g, ln2_b)


def setup_inputs(seed: int = 0) -> dict:
    key = jax.random.key(seed)
    ks = jax.random.split(key, 24)
    f32 = jnp.float32
    nrm = lambda k, shape, s: jax.random.normal(k, shape, f32) * s
    L, D = DEPTH, D_MODEL
    return {
        "x_prompt": nrm(ks[0], (BATCH, SEQ, D), 1.0),
        "x_sample": nrm(ks[1], (DEC_BATCH, DEC_SEQ, D), 1.0),
        "w_in": nrm(ks[2], (L, D, D_IN), D ** -0.5),
        "lam_q1": nrm(ks[3], (L, DA_DK), 0.1),
        "lam_k1": nrm(ks[4], (L, DA_DK), 0.1),
        "lam_q2": nrm(ks[5], (L, DA_DK), 0.1),
        "lam_k2": nrm(ks[6], (L, DA_DK), 0.1),
        "subln_g": 1.0 + nrm(ks[7], (L, DA_DV), 0.02),
        "w_dec_f": nrm(ks[8], (L, GLA_RANK, GLA_QK_WIDTH), GLA_RANK ** -0.5),
        "b_dec_f": nrm(ks[9], (L, GLA_QK_WIDTH), 0.02),
        "w_dec_b": nrm(ks[10], (L, GLA_RANK, GLA_QK_WIDTH), GLA_RANK ** -0.5),
        "b_dec_b": nrm(ks[11], (L, GLA_QK_WIDTH), 0.02),
        "gla_norm_g": 1.0 + nrm(ks[12], (L, GLA_DV), 0.02),
        "w_br_a": nrm(ks[13], (L, DA_WIDTH, D), DA_WIDTH ** -0.5),
        "w_br_b": nrm(ks[14], (L, GLA_WIDTH, D), GLA_WIDTH ** -0.5),
        "b_gate": nrm(ks[15], (L, 2 * D), 0.02),
        "w_out": nrm(ks[16], (L, D, D), BETA * D ** -0.5),
        "ln1_g": 1.0 + nrm(ks[17], (L, D), 0.02),
        "ln1_b": nrm(ks[18], (L, D), 0.02),
        "w_mlp1": nrm(ks[19], (L, D, D_FF), D ** -0.5),
        "w_mlp2": nrm(ks[20], (L, D_FF, D), BETA * D_FF ** -0.5),
        "ln2_g": 1.0 + nrm(ks[21], (L, D), 0.02),
        "ln2_b": nrm(ks[22], (L, D), 0.02),
    }


def reference(x_prompt, x_sample, w_in, lam_q1, lam_k1, lam_q2, lam_k2, subln_g, w_dec_f, b_dec_f,
              w_dec_b, b_dec_b, gla_norm_g, w_br_a, w_br_b, b_gate, w_out,
              ln1_g, ln1_b, w_mlp1, w_mlp2, ln2_g, ln2_b):
    def trunk(x):
        for l in range(DEPTH):
            x = encoder_layer(x, l, w_in[l], lam_q1[l], lam_k1[l], lam_q2[l], lam_k2[l], subln_g[l],
                              w_dec_f[l], b_dec_f[l], w_dec_b[l], b_dec_b[l], gla_norm_g[l],
                              w_br_a[l], w_br_b[l], b_gate[l], w_out[l],
                              ln1_g[l], ln1_b[l], w_mlp1[l], w_mlp2[l], ln2_g[l], ln2_b[l])
        return x

    y_prompt = trunk(x_prompt)
    y_sample = trunk(x_sample)
    return (y_prompt, y_sample)
```

```python
import functools
import math

import jax
import jax.numpy as jnp
from jax import lax
from jax.experimental import pallas as pl
from jax.experimental.pallas import tpu as pltpu

F32 = jnp.float32
BF16 = jnp.bfloat16

D_MODEL = 1024
DA_HEADS = 4
DA_DK = 64
DA_DV = 128
DA_WIDTH = 512
ROT_DIM = 16
ROPE_THETA = 500000.0
GLA_HEADS = 4
GLA_DK = 64
GLA_DV = 128
GLA_QK = 256
GLA_WIDTH = 512
GLA_RANK = 16
GLA_TAU = 16.0
D_FF = 4096
LN_EPS = 1e-5
RMS_EPS = 1e-6
IN_SIZES = (512, 512, 512, 256, 256, 512, 512, 16, 16, 2048)

LANES = 128
VMEM_LIMIT = 56 * 1024 * 1024

PROJ_TM = 512
ATT_TQ = 128
ATT_TK = 256
GLA_C = 128
MERGE_TM = 512
MLP_TM = 512


def _dot(a, b):
    return jnp.dot(a, b, preferred_element_type=F32)


def _dot_nt(a, b):
    return lax.dot_general(a, b, (((1,), (1,)), ((), ())), preferred_element_type=F32)


def _dot_tn(a, b):
    return lax.dot_general(a, b, (((0,), (0,)), ((), ())), preferred_element_type=F32)


def _proj_kernel(x_ref, wq_ref, wk_ref, wv_ref, wgq_ref, wgk_ref, wgv_ref, wgr_ref, wz_ref,
                 wdec_ref, bdec_ref, cos_ref, sa_ref, sb_ref,
                 q_ref, k_ref, vt_ref, gq_ref, gk_ref, gv_ref, rs_ref, gf_ref, gb_ref):
    x = x_ref[...].astype(BF16)
    cos = cos_ref[...]
    sin_a = sa_ref[...]
    sin_b = sb_ref[...]

    def rope(t):
        cols = []
        for j in range(t.shape[1] // LANES):
            blk = t[:, j * LANES:(j + 1) * LANES]
            cols.append(blk * cos + pltpu.roll(blk, LANES - ROT_DIM // 2, 1) * sin_a
                        + pltpu.roll(blk, ROT_DIM // 2, 1) * sin_b)
        return jnp.concatenate(cols, axis=1)

    q_ref[...] = rope(_dot(x, wq_ref[...])).astype(BF16)
    k_ref[...] = rope(_dot(x, wk_ref[...])).astype(BF16)
    vt_ref[...] = _dot(x, wv_ref[...]).T.astype(BF16)
    gq_ref[...] = _dot(x, wgq_ref[...]).astype(BF16)
    gk_ref[...] = _dot(x, wgk_ref[...]).astype(BF16)
    gv_ref[...] = _dot(x, wgv_ref[...]).astype(BF16)
    r = _dot(x, wgr_ref[...])
    rs_ref[...] = (r * jax.nn.sigmoid(r)).astype(BF16)
    z = _dot(x, wz_ref[...])
    zz = jnp.dot(z, wdec_ref[...], precision=lax.Precision.HIGHEST,
                 preferred_element_type=F32) + bdec_ref[...]
    g = -(jnp.maximum(-zz, 0.0) + jnp.log1p(jnp.exp(-jnp.abs(zz)))) / GLA_TAU
    gf_ref[...] = g[:, :GLA_QK]
    gb_ref[...] = g[:, GLA_QK:]


def _proj_call(x2, B, S, w, tabs):
    M = x2.shape[0]
    tm = PROJ_TM
    ns = S // tm
    full = lambda a: pl.BlockSpec(a.shape, lambda i: (0,) * a.ndim)
    tok = lambda n: pl.BlockSpec((tm, n), lambda i: (i, 0))
    tab = pl.BlockSpec((tm, LANES), lambda i: (i % ns, 0))
    ins = [x2, w["wq"], w["wk"], w["wv"], w["wgq"], w["wgk"], w["wgv"], w["wgr"], w["wz"],
           w["wdec"], w["bdec"], tabs[0], tabs[1], tabs[2]]
    in_specs = [tok(D_MODEL)] + [full(a) for a in ins[1:11]] + [tab, tab, tab]
    out_shape = [
        jax.ShapeDtypeStruct((M, 512), BF16),
        jax.ShapeDtypeStruct((M, 512), BF16),
        jax.ShapeDtypeStruct((B, 512, S), BF16),
        jax.ShapeDtypeStruct((M, GLA_QK), BF16),
        jax.ShapeDtypeStruct((M, GLA_QK), BF16),
        jax.ShapeDtypeStruct((M, GLA_WIDTH), BF16),
        jax.ShapeDtypeStruct((M, GLA_WIDTH), BF16),
        jax.ShapeDtypeStruct((M, GLA_QK), F32),
        jax.ShapeDtypeStruct((M, GLA_QK), F32),
    ]
    out_specs = [tok(512), tok(512),
                 pl.BlockSpec((None, 512, tm), lambda i: (i // ns, 0, i % ns)),
                 tok(GLA_QK), tok(GLA_QK), tok(GLA_WIDTH), tok(GLA_WIDTH), tok(GLA_QK), tok(GLA_QK)]
    return pl.pallas_call(
        _proj_kernel, grid=(M // tm,), in_specs=in_specs, out_specs=out_specs, out_shape=out_shape,
        compiler_params=pltpu.CompilerParams(dimension_semantics=("parallel",),
                                             vmem_limit_bytes=VMEM_LIMIT),
        name="proj")(*ins)


def _attn_kernel(lamv_ref, g_ref, q_ref, k_ref, vt_ref, o_ref, *, seq, lam_init):
    tq, tk = ATT_TQ, ATT_TK
    qt = q_ref[...].astype(F32).T
    row = lax.broadcasted_iota(jnp.int32, qt.shape, 0)
    q0 = jnp.where(row < DA_DK, qt, 0.0)
    q1 = jnp.where(row >= DA_DK, qt, 0.0)
    qq = jnp.concatenate([q0, q1], axis=1).astype(BF16)

    def body(j, carry):
        m, l, acc = carry
        off = pl.multiple_of(j * tk, tk)
        s = _dot(k_ref[pl.ds(off, tk), :], qq)
        m_new = jnp.maximum(m, jnp.max(s, axis=0, keepdims=True))
        alpha = jnp.exp(m - m_new)
        p = jnp.exp(s - m_new)
        l = alpha * l + jnp.sum(p, axis=0, keepdims=True)
        acc = alpha * acc + _dot(vt_ref[:, pl.ds(off, tk)], p.astype(BF16))
        return m_new, l, acc

    m0 = jnp.full((1, 2 * tq), -jnp.inf, F32)
    l0 = jnp.zeros((1, 2 * tq), F32)
    a0 = jnp.zeros((DA_DV, 2 * tq), F32)
    _, l, acc = lax.fori_loop(0, seq // tk, body, (m0, l0, a0))

    lv = lamv_ref[...]
    lam = (jnp.exp(jnp.sum(lv[0:1] * lv[1:2], keepdims=True))
           - jnp.exp(jnp.sum(lv[2:3] * lv[3:4], keepdims=True)) + lam_init)
    on = acc / l
    o = on[:, :tq] - lam * on[:, tq:]
    ms = jnp.mean(o * o, axis=0, keepdims=True)
    y = o * lax.rsqrt(ms + RMS_EPS) * g_ref[...] * (1.0 - lam_init)
    o_ref[...] = y.T.astype(BF16)


def _attn_call(q, k, vt, lamv, subg, lam_init):
    B, S, _ = q.shape
    tq = ATT_TQ
    return pl.pallas_call(
        functools.partial(_attn_kernel, seq=S, lam_init=lam_init),
        grid=(B, DA_HEADS, S // tq),
        in_specs=[
            pl.BlockSpec(lamv.shape, lambda b, h, i: (0, 0)),
            pl.BlockSpec(subg.shape, lambda b, h, i: (0, 0)),
            pl.BlockSpec((None, tq, LANES), lambda b, h, i: (b, i, h)),
            pl.BlockSpec((None, S, LANES), lambda b, h, i: (b, 0, h)),
            pl.BlockSpec((None, LANES, S), lambda b, h, i: (b, h, 0)),
        ],
        out_specs=pl.BlockSpec((None, tq, LANES), lambda b, h, i: (b, i, h)),
        out_shape=jax.ShapeDtypeStruct((B, S, DA_WIDTH), BF16),
        compiler_params=pltpu.CompilerParams(
            dimension_semantics=("parallel", "parallel", "arbitrary"),
            vmem_limit_bytes=VMEM_LIMIT),
        name="diff_attn")(lamv, subg, q, k, vt)


def _cumsum_rows(tri, g):
    g1 = g.astype(BF16)
    r1 = g - g1.astype(F32)
    g2 = r1.astype(BF16)
    g3 = (r1 - g2.astype(F32)).astype(BF16)
    return _dot(tri, g1) + _dot(tri, g2) + _dot(tri, g3)


def _pair_ref_rows(c, w, rowi):
    C, n = c.shape
    if w >= 8:
        c3 = c.reshape(C // (2 * w), 2 * w, n)
        return jnp.broadcast_to(c3[:, w:w + 1, :], c3.shape).reshape(C, n)
    rem = rowi & (2 * w - 1)
    out = c
    for r in range(2 * w):
        if r == w:
            continue
        out = jnp.where(rem == r, pltpu.roll(c, (r - w) % C, 0), out)
    return out


def _gla_kernel(q_ref, k_ref, v_ref, gf_ref, gb_ref, qb_ref, kb_ref, vb_ref, gbb_ref,
                om_ref, ob_ref, sf_ref, sb_ref):
    C = GLA_C
    step = pl.program_id(1)

    @pl.when(step == 0)
    def _():
        sf_ref[...] = jnp.zeros_like(sf_ref)
        sb_ref[...] = jnp.zeros_like(sb_ref)

    ri = lax.broadcasted_iota(jnp.int32, (C, C), 0)
    ci = lax.broadcasted_iota(jnp.int32, (C, C), 1)
    tri = (ci <= ri).astype(BF16)
    x = ri ^ ci
    x = x | (x >> 1)
    x = x | (x >> 2)
    x = x | (x >> 4)
    hb = x - (x >> 1)
    rowi = lax.broadcasted_iota(jnp.int32, (C, 1), 0)
    lane = lax.broadcasted_iota(jnp.int32, (C, LANES), 1)
    head_lo = lane < GLA_DK
    sr = lax.broadcasted_iota(jnp.int32, (2 * GLA_DV, 2 * GLA_DK), 0)
    sc = lax.broadcasted_iota(jnp.int32, (2 * GLA_DV, 2 * GLA_DK), 1)
    sdiag = (sr < GLA_DV) == (sc < GLA_DK)

    q = q_ref[...].astype(F32)
    k = k_ref[...].astype(F32)
    v = v_ref[...]
    gf = gf_ref[...]
    gb = gb_ref[...]
    cf = _cumsum_rows(tri, gf)
    cb = _cumsum_rows(tri, gb) - gb

    a = [jnp.zeros((C, C), F32) for _ in range(GLA_HEADS)]
    w = C // 2
    while w >= 1:
        odd = (rowi & w) != 0
        d_f = cf - _pair_ref_rows(cf, w, rowi)
        d_b = cb - _pair_ref_rows(cb, w, rowi)
        qt = (q * jnp.exp(jnp.where(odd, d_f, -d_b))).astype(BF16)
        kt = (k * jnp.exp(jnp.where(odd, d_b, -d_f))).astype(BF16)
        sel = hb == w
        for h in range(GLA_HEADS):
            pr = slice((h // 2) * LANES, (h // 2 + 1) * LANES)
            keep = head_lo if h % 2 == 0 else jnp.logical_not(head_lo)
            lhs = jnp.where(keep, qt[:, pr], jnp.zeros((), BF16))
            a[h] = a[h] + jnp.where(sel, _dot_nt(lhs, kt[:, pr]), 0.0)
        w //= 2
    qh = q_ref[...]
    kh = k_ref[...]
    for h in range(GLA_HEADS):
        pr = slice((h // 2) * LANES, (h // 2 + 1) * LANES)
        keep = head_lo if h % 2 == 0 else jnp.logical_not(head_lo)
        lhs = jnp.where(keep, qh[:, pr], jnp.zeros((), BF16))
        a[h] = a[h] + jnp.where(hb == 0, 2.0 * _dot_nt(lhs, kh[:, pr]), 0.0)

    cf_last = cf[C - 1:C, :]
    qe = (q * jnp.exp(cf)).astype(BF16)
    ke = (k * jnp.exp(cf_last - cf)).astype(BF16)
    dec = jnp.exp(cf_last)
    outs = []
    for p in range(2):
        pr = slice(p * LANES, (p + 1) * LANES)
        vr = slice(p * 2 * GLA_DV, (p + 1) * 2 * GLA_DV)
        st = sf_ref[p]
        inter = _dot_nt(qe[:, pr], st.astype(BF16))
        for hh in range(2):
            h = 2 * p + hh
            vh = slice(h * GLA_DV, (h + 1) * GLA_DV)
            outs.append(_dot(a[h].astype(BF16), v[:, vh]) + inter[:, hh * GLA_DV:(hh + 1) * GLA_DV])
        upd = _dot_tn(v[:, vr], ke[:, pr])
        sf_ref[p] = dec[:, pr] * st + jnp.where(sdiag, upd, 0.0)
    om_ref[...] = jnp.concatenate(outs, axis=1)

    q2 = qb_ref[...].astype(F32)
    k2 = kb_ref[...].astype(F32)
    v2 = vb_ref[...]
    g2 = gbb_ref[...]
    c2 = _cumsum_rows(tri, g2)
    tot = c2[C - 1:C, :]
    cx = c2 - g2
    qe2 = (q2 * jnp.exp(tot - cx)).astype(BF16)
    ke2 = (k2 * jnp.exp(cx)).astype(BF16)
    dec2 = jnp.exp(tot)
    outs = []
    for p in range(2):
        pr = slice(p * LANES, (p + 1) * LANES)
        vr = slice(p * 2 * GLA_DV, (p + 1) * 2 * GLA_DV)
        st = sb_ref[p]
        outs.append(_dot_nt(qe2[:, pr], st.astype(BF16)))
        upd = _dot_tn(v2[:, vr], ke2[:, pr])
        sb_ref[p] = dec2[:, pr] * st + jnp.where(sdiag, upd, 0.0)
    ob_ref[...] = jnp.concatenate(outs, axis=1)


def _gla_call(gq, gk, gv, gf, gb, B, S):
    M = gq.shape[0]
    C = GLA_C
    n = S // C
    fw = lambda width: pl.BlockSpec((C, width), lambda b, i: (b * n + i, 0))
    bw = lambda width: pl.BlockSpec((C, width), lambda b, i: (b * n + n - 1 - i, 0))
    return pl.pallas_call(
        _gla_kernel, grid=(B, n),
        in_specs=[fw(GLA_QK), fw(GLA_QK), fw(GLA_WIDTH), fw(GLA_QK), fw(GLA_QK),
                  bw(GLA_QK), bw(GLA_QK), bw(GLA_WIDTH), bw(GLA_QK)],
        out_specs=[fw(GLA_WIDTH), bw(GLA_WIDTH)],
        out_shape=[jax.ShapeDtypeStruct((M, GLA_WIDTH), F32)] * 2,
        scratch_shapes=[pltpu.VMEM((2, 2 * GLA_DV, 2 * GLA_DK), F32)] * 2,
        compiler_params=pltpu.CompilerParams(dimension_semantics=("parallel", "arbitrary"),
                                             vmem_limit_bytes=VMEM_LIMIT),
        name="gla")(gq, gk, gv, gf, gb, gq, gk, gv, gb)


def _layer_norm(h, g, b):
    mu = jnp.mean(h, axis=-1, keepdims=True)
    d = h - mu
    var = jnp.mean(d * d, axis=-1, keepdims=True)
    return d * lax.rsqrt(var + LN_EPS) * g + b


def _merge_kernel(x_ref, ya_ref, om_ref, ob_ref, rs_ref, wg_ref, bg_ref, gn_ref,
                  wa_ref, wb_ref, wo_ref, lg_ref, lb_ref, o_ref, *, alpha):
    x = x_ref[...]
    gates = jax.nn.sigmoid(_dot(x.astype(BF16), wg_ref[...]) + bg_ref[...])
    o = om_ref[...] + ob_ref[...]
    gn = gn_ref[...]
    cols = []
    for h in range(GLA_HEADS):
        oh = o[:, h * GLA_DV:(h + 1) * GLA_DV]
        ms = jnp.mean(oh * oh, axis=-1, keepdims=True)
        cols.append(oh * lax.rsqrt(ms + RMS_EPS) * gn)
    yb = (jnp.concatenate(cols, axis=1) * rs_ref[...].astype(F32)).astype(BF16)
    pa = _dot(ya_ref[...], wa_ref[...])
    pb = _dot(yb, wb_ref[...])
    mixin = gates[:, :D_MODEL] * pa + gates[:, D_MODEL:] * pb
    mix = _dot(mixin.astype(BF16), wo_ref[...])
    o_ref[...] = _layer_norm(alpha * x + mix, lg_ref[...], lb_ref[...])


def _merge_call(x2, ya, om, ob, rs, w, alpha):
    M = x2.shape[0]
    tm = MERGE_TM
    full = lambda a: pl.BlockSpec(a.shape, lambda i: (0,) * a.ndim)
    tok = lambda n: pl.BlockSpec((tm, n), lambda i: (i, 0))
    ws = [w["wgate"], w["bgate"], w["gnorm"], w["wbra"], w["wbrb"], w["wout"], w["ln1g"], w["ln1b"]]
    return pl.pallas_call(
        functools.partial(_merge_kernel, alpha=alpha), grid=(M // tm,),
        in_specs=[tok(D_MODEL), tok(512), tok(512), tok(512), tok(512)] + [full(a) for a in ws],
        out_specs=tok(D_MODEL), out_shape=jax.ShapeDtypeStruct((M, D_MODEL), F32),
        compiler_params=pltpu.CompilerParams(dimension_semantics=("parallel",),
                                             vmem_limit_bytes=VMEM_LIMIT),
        name="merge")(x2, ya, om, ob, rs, *ws)


def _mlp_kernel(x_ref, w1_ref, w2_ref, lg_ref, lb_ref, o_ref, *, alpha):
    x = x_ref[...]
    h = jnp.maximum(_dot(x.astype(BF16), w1_ref[...]), 0.0)
    h = _dot((h * h).astype(BF16), w2_ref[...])
    o_ref[...] = _layer_norm(alpha * x + h, lg_ref[...], lb_ref[...])


def _mlp_call(x1, w, alpha):
    M = x1.shape[0]
    tm = MLP_TM
    once = lambda a: pl.BlockSpec(a.shape, lambda i: (0,) * a.ndim, pipeline_mode=pl.Buffered(1))
    tok = pl.BlockSpec((tm, D_MODEL), lambda i: (i, 0))
    ws = [w["w1"], w["w2"], w["ln2g"], w["ln2b"]]
    return pl.pallas_call(
        functools.partial(_mlp_kernel, alpha=alpha), grid=(M // tm,),
        in_specs=[tok] + [once(a) for a in ws],
        out_specs=tok, out_shape=jax.ShapeDtypeStruct((M, D_MODEL), F32),
        compiler_params=pltpu.CompilerParams(dimension_semantics=("parallel",),
                                             vmem_limit_bytes=VMEM_LIMIT),
        name="mlp")(x1, *ws)


def _rope_tables(S):
    half = ROT_DIM // 2
    inv = 1.0 / (ROPE_THETA ** (jnp.arange(0, ROT_DIM, 2, dtype=F32) / ROT_DIM))
    ang = jnp.arange(S, dtype=F32)[:, None] * inv[None, :]
    cos, sin = jnp.cos(ang), jnp.sin(ang)
    d = jnp.arange(LANES) % DA_DK
    idx = d % half
    cos_t = jnp.where(d[None, :] < ROT_DIM, cos[:, idx], 1.0)
    sin_a = jnp.where(d[None, :] < half, -sin[:, idx], 0.0)
    sin_b = jnp.where((d[None, :] >= half) & (d[None, :] < ROT_DIM), sin[:, idx], 0.0)
    return cos_t.astype(F32), sin_a.astype(F32), sin_b.astype(F32)


def _layer_weights(l, w_in, subln_g, w_dec_f, b_dec_f, w_dec_b, b_dec_b, gla_norm_g, w_br_a, w_br_b,
                   b_gate, w_out, ln1_g, ln1_b, w_mlp1, w_mlp2, ln2_g, ln2_b):
    offs = [0]
    for s in IN_SIZES:
        offs.append(offs[-1] + s)
    col = lambda i: w_in[l][:, offs[i]:offs[i + 1]]
    zeros = jnp.zeros((GLA_RANK, GLA_QK), F32)
    row = lambda a: a[l].reshape(1, -1).astype(F32)
    return {
        "wq": (col(0) * DA_DK ** -0.5).astype(BF16), "wk": col(1).astype(BF16), "wv": col(2).astype(BF16),
        "wgq": (col(3) * GLA_DK ** -0.5).astype(BF16), "wgk": col(4).astype(BF16),
        "wgv": col(5).astype(BF16), "wgr": col(6).astype(BF16),
        "wz": jnp.concatenate([col(7), col(8)], axis=1).astype(BF16),
        "wdec": jnp.concatenate([jnp.concatenate([w_dec_f[l], zeros], axis=1),
                                 jnp.concatenate([zeros, w_dec_b[l]], axis=1)], axis=0).astype(F32),
        "bdec": jnp.concatenate([b_dec_f[l], b_dec_b[l]]).reshape(1, -1).astype(F32),
        "wgate": col(9).astype(BF16), "bgate": row(b_gate),
        "subg": subln_g[l].reshape(-1, 1).astype(F32), "gnorm": row(gla_norm_g),
        "wbra": w_br_a[l].astype(BF16), "wbrb": w_br_b[l].astype(BF16), "wout": w_out[l].astype(BF16),
        "ln1g": row(ln1_g), "ln1b": row(ln1_b),
        "w1": w_mlp1[l].astype(BF16), "w2": w_mlp2[l].astype(BF16),
        "ln2g": row(ln2_g), "ln2b": row(ln2_b),
    }


def kernel(x_prompt, x_sample, w_in, lam_q1, lam_k1, lam_q2, lam_k2, subln_g, w_dec_f, b_dec_f, w_dec_b, b_dec_b, gla_norm_g, w_br_a, w_br_b, b_gate, w_out, ln1_g, ln1_b, w_mlp1, w_mlp2, ln2_g, ln2_b):
    depth = w_in.shape[0]
    alpha = (2 * depth) ** 0.25
    layers = []
    for l in range(depth):
        w = _layer_weights(l, w_in, subln_g, w_dec_f, b_dec_f, w_dec_b, b_dec_b, gla_norm_g,
                           w_br_a, w_br_b, b_gate, w_out, ln1_g, ln1_b, w_mlp1, w_mlp2, ln2_g, ln2_b)
        lamv = jnp.stack([lam_q1[l], lam_k1[l], lam_q2[l], lam_k2[l]]).astype(F32)
        layers.append((w, lamv, 0.8 - 0.6 * math.exp(-0.3 * l)))

    def trunk(x):
        B, S, D = x.shape
        tabs = _rope_tables(S)
        x2 = x.reshape(B * S, D)
        for w, lamv, lam_init in layers:
            q, k, vt, gq, gk, gv, rs, gf, gb = _proj_call(x2, B, S, w, tabs)
            ya = _attn_call(q.reshape(B, S, 512), k.reshape(B, S, 512), vt, lamv, w["subg"], lam_init)
            om, ob = _gla_call(gq, gk, gv, gf, gb, B, S)
            x1 = _merge_call(x2, ya.reshape(B * S, 512), om, ob, rs, w, alpha)
            x2 = _mlp_call(x1, w, alpha)
        return x2.reshape(B, S, D)

    return (trunk(x_prompt), trunk(x_sample))
```

```python
import functools
import math

import jax
import jax.numpy as jnp
from jax import lax
from jax.experimental import pallas as pl
from jax.experimental.pallas import tpu as pltpu

F32 = jnp.float32
BF16 = jnp.bfloat16

D_MODEL = 1024
DA_HEADS = 4
DA_DK = 64
DA_DV = 128
DA_WIDTH = 512
ROT_DIM = 16
ROPE_THETA = 500000.0
GLA_HEADS = 4
GLA_DK = 64
GLA_DV = 128
GLA_QK = 256
GLA_WIDTH = 512
GLA_RANK = 16
GLA_TAU = 16.0
D_FF = 4096
LN_EPS = 1e-5
RMS_EPS = 1e-6
IN_SIZES = (512, 512, 512, 256, 256, 512, 512, 16, 16, 2048)

LANES = 128
VMEM_LIMIT = 56 * 1024 * 1024

PROJ_TM = 512
ATT_TQ = 128
ATT_TK = 1024
ATT_RC = 128
ATT_PV_CHUNKS = 2
LOG2E = math.log2(math.e)
GLA_C = 128
MERGE_TM = 512
MLP_TM = 512


def _dot(a, b):
    return jnp.dot(a, b, preferred_element_type=F32)


def _dot_nt(a, b):
    return lax.dot_general(a, b, (((1,), (1,)), ((), ())), preferred_element_type=F32)


def _dot_tn(a, b):
    return lax.dot_general(a, b, (((0,), (0,)), ((), ())), preferred_element_type=F32)


def _proj_kernel(x_ref, wq_ref, wk_ref, wv_ref, wgq_ref, wgk_ref, wgv_ref, wgr_ref, wz_ref,
                 wdec_ref, bdec_ref, cos_ref, sa_ref, sb_ref,
                 q_ref, k_ref, vt_ref, gq_ref, gk_ref, gv_ref, rs_ref, gf_ref, gb_ref):
    x = x_ref[...].astype(BF16)
    cos = cos_ref[...]
    sin_a = sa_ref[...]
    sin_b = sb_ref[...]

    def rope(t):
        cols = []
        for j in range(t.shape[1] // LANES):
            blk = t[:, j * LANES:(j + 1) * LANES]
            cols.append(blk * cos + pltpu.roll(blk, LANES - ROT_DIM // 2, 1) * sin_a
                        + pltpu.roll(blk, ROT_DIM // 2, 1) * sin_b)
        return jnp.concatenate(cols, axis=1)

    q_ref[...] = rope(_dot(x, wq_ref[...])).astype(BF16)
    k_ref[...] = rope(_dot(x, wk_ref[...])).astype(BF16)
    vt_ref[...] = _dot(x, wv_ref[...]).T.astype(BF16)
    gq_ref[...] = _dot(x, wgq_ref[...]).astype(BF16)
    gk_ref[...] = _dot(x, wgk_ref[...]).astype(BF16)
    gv_ref[...] = _dot(x, wgv_ref[...]).astype(BF16)
    r = _dot(x, wgr_ref[...])
    rs_ref[...] = (r * jax.nn.sigmoid(r)).astype(BF16)
    z = _dot(x, wz_ref[...])
    zz = jnp.dot(z, wdec_ref[...], precision=lax.Precision.HIGHEST,
                 preferred_element_type=F32) + bdec_ref[...]
    g = -(jnp.maximum(-zz, 0.0) + jnp.log1p(jnp.exp(-jnp.abs(zz)))) / GLA_TAU
    gf_ref[...] = g[:, :GLA_QK]
    gb_ref[...] = g[:, GLA_QK:]


def _proj_call(x2, B, S, w, tabs):
    M = x2.shape[0]
    tm = PROJ_TM
    ns = S // tm
    full = lambda a: pl.BlockSpec(a.shape, lambda i: (0,) * a.ndim)
    tok = lambda n: pl.BlockSpec((tm, n), lambda i: (i, 0))
    tab = pl.BlockSpec((tm, LANES), lambda i: (i % ns, 0))
    ins = [x2, w["wq"], w["wk"], w["wv"], w["wgq"], w["wgk"], w["wgv"], w["wgr"], w["wz"],
           w["wdec"], w["bdec"], tabs[0], tabs[1], tabs[2]]
    in_specs = [tok(D_MODEL)] + [full(a) for a in ins[1:11]] + [tab, tab, tab]
    out_shape = [
        jax.ShapeDtypeStruct((M, 512), BF16),
        jax.ShapeDtypeStruct((M, 512), BF16),
        jax.ShapeDtypeStruct((B, 512, S), BF16),
        jax.ShapeDtypeStruct((M, GLA_QK), BF16),
        jax.ShapeDtypeStruct((M, GLA_QK), BF16),
        jax.ShapeDtypeStruct((M, GLA_WIDTH), BF16),
        jax.ShapeDtypeStruct((M, GLA_WIDTH), BF16),
        jax.ShapeDtypeStruct((M, GLA_QK), F32),
        jax.ShapeDtypeStruct((M, GLA_QK), F32),
    ]
    out_specs = [tok(512), tok(512),
                 pl.BlockSpec((None, 512, tm), lambda i: (i // ns, 0, i % ns)),
                 tok(GLA_QK), tok(GLA_QK), tok(GLA_WIDTH), tok(GLA_WIDTH), tok(GLA_QK), tok(GLA_QK)]
    return pl.pallas_call(
        _proj_kernel, grid=(M // tm,), in_specs=in_specs, out_specs=out_specs, out_shape=out_shape,
        compiler_params=pltpu.CompilerParams(dimension_semantics=("parallel",),
                                             vmem_limit_bytes=VMEM_LIMIT),
        name="proj")(*ins)


def _attn_kernel(lamv_ref, g_ref, q_ref, k_ref, vt_ref, o_ref, s0_ref, s1_ref, p0_ref, p1_ref,
                 *, seq, lam_init):
    tq, tk, rc = ATT_TQ, ATT_TK, ATT_RC
    ncol = 2 * tq
    nch = tk // rc
    qt = q_ref[...].astype(F32).T
    row = lax.broadcasted_iota(jnp.int32, qt.shape, 0)
    q0 = jnp.where(row < DA_DK, qt, 0.0)
    q1 = jnp.where(row >= DA_DK, qt, 0.0)
    qq = jnp.concatenate([q0, q1], axis=1).astype(BF16)

    def tile_step(j_next, s_next, j_cur, s_cur, p_ref, mx_cur, m, l, acc):
        mx_next = None
        if j_cur is not None:
            m_new = jnp.maximum(m, jnp.max(mx_cur, axis=0, keepdims=True))
            alpha = jnp.exp2(m - m_new)
            mb = jnp.broadcast_to(m_new, (8, ncol))[None]
            ls = None
            pv = None
        for c in range(nch):
            rows = slice(c * rc, (c + 1) * rc)
            if j_next is not None:
                off = pl.multiple_of(j_next * tk + c * rc, rc)
                sc = _dot(k_ref[pl.ds(off, rc), :], qq)
                s_next[rows, :] = sc
                part = jnp.max(sc.reshape(rc // 8, 8, ncol), axis=0)
                mx_next = part if mx_next is None else jnp.maximum(mx_next, part)
            if j_cur is not None:
                p = jnp.exp2(s_cur[rows, :].reshape(rc // 8, 8, ncol) - mb)
                part = jnp.sum(p, axis=0)
                ls = part if ls is None else ls + part
                p_ref[rows, :] = p.reshape(rc, ncol).astype(BF16)
                if (c + 1) % ATT_PV_CHUNKS == 0:
                    lo = (c + 1 - ATT_PV_CHUNKS) * rc
                    off = pl.multiple_of(j_cur * tk + lo, ATT_PV_CHUNKS * rc)
                    d = _dot(vt_ref[:, pl.ds(off, ATT_PV_CHUNKS * rc)], p_ref[lo:(c + 1) * rc, :])
                    pv = d if pv is None else pv + d
        if j_cur is not None:
            l = alpha * l + jnp.sum(ls, axis=0, keepdims=True)
            acc = alpha * acc + pv
            m = m_new
        return mx_next, m, l, acc

    def body(i, carry):
        mx0, m, l, acc = carry
        j = 2 * i
        mx1, m, l, acc = tile_step(j + 1, s1_ref, j, s0_ref, p0_ref, mx0, m, l, acc)
        mx0, m, l, acc = tile_step(j + 2, s0_ref, j + 1, s1_ref, p1_ref, mx1, m, l, acc)
        return mx0, m, l, acc

    m0 = jnp.full((1, ncol), -jnp.inf, F32)
    l0 = jnp.zeros((1, ncol), F32)
    a0 = jnp.zeros((DA_DV, ncol), F32)
    n_tiles = seq // tk
    mx0, _, _, _ = tile_step(0, s0_ref, None, None, None, None, m0, l0, a0)
    mx0, m, l, acc = lax.fori_loop(0, n_tiles // 2 - 1, body, (mx0, m0, l0, a0))
    mx1, m, l, acc = tile_step(n_tiles - 1, s1_ref, n_tiles - 2, s0_ref, p0_ref, mx0, m, l, acc)
    _, _, l, acc = tile_step(None, None, n_tiles - 1, s1_ref, p1_ref, mx1, m, l, acc)

    lv = lamv_ref[...]
    lam = (jnp.exp(jnp.sum(lv[0:1] * lv[1:2], keepdims=True))
           - jnp.exp(jnp.sum(lv[2:3] * lv[3:4], keepdims=True)) + lam_init)
    on = acc / l
    o = on[:, :tq] - lam * on[:, tq:]
    ms = jnp.mean(o * o, axis=0, keepdims=True)
    y = o * lax.rsqrt(ms + RMS_EPS) * g_ref[...] * (1.0 - lam_init)
    o_ref[...] = y.T.astype(BF16)


def _attn_call(q, k, vt, lamv, subg, lam_init):
    B, S, _ = q.shape
    tq = ATT_TQ
    return pl.pallas_call(
        functools.partial(_attn_kernel, seq=S, lam_init=lam_init),
        grid=(B, DA_HEADS, S // tq),
        in_specs=[
            pl.BlockSpec(lamv.shape, lambda b, h, i: (0, 0)),
            pl.BlockSpec(subg.shape, lambda b, h, i: (0, 0)),
            pl.BlockSpec((None, tq, LANES), lambda b, h, i: (b, i, h)),
            pl.BlockSpec((None, S, LANES), lambda b, h, i: (b, 0, h)),
            pl.BlockSpec((None, LANES, S), lambda b, h, i: (b, h, 0)),
        ],
        out_specs=pl.BlockSpec((None, tq, LANES), lambda b, h, i: (b, i, h)),
        out_shape=jax.ShapeDtypeStruct((B, S, DA_WIDTH), BF16),
        scratch_shapes=[pltpu.VMEM((ATT_TK, 2 * tq), F32)] * 2 + [pltpu.VMEM((ATT_TK, 2 * tq), BF16)] * 2,
        compiler_params=pltpu.CompilerParams(
            dimension_semantics=("parallel", "parallel", "arbitrary"),
            vmem_limit_bytes=VMEM_LIMIT),
        name="diff_attn")(lamv, subg, q, k, vt)


def _cumsum_rows(tri, g):
    g1 = g.astype(BF16)
    r1 = g - g1.astype(F32)
    g2 = r1.astype(BF16)
    g3 = (r1 - g2.astype(F32)).astype(BF16)
    return _dot(tri, g1) + _dot(tri, g2) + _dot(tri, g3)


def _pair_ref_rows(c, w, rowi):
    C, n = c.shape
    if w >= 8:
        c3 = c.reshape(C // (2 * w), 2 * w, n)
        return jnp.broadcast_to(c3[:, w:w + 1, :], c3.shape).reshape(C, n)
    rem = rowi & (2 * w - 1)
    out = c
    for r in range(2 * w):
        if r == w:
            continue
        out = jnp.where(rem == r, pltpu.roll(c, (r - w) % C, 0), out)
    return out


def _gla_kernel(q_ref, k_ref, v_ref, gf_ref, gb_ref, qb_ref, kb_ref, vb_ref, gbb_ref,
                om_ref, ob_ref, sf_ref, sb_ref):
    C = GLA_C
    step = pl.program_id(1)

    @pl.when(step == 0)
    def _():
        sf_ref[...] = jnp.zeros_like(sf_ref)
        sb_ref[...] = jnp.zeros_like(sb_ref)

    ri = lax.broadcasted_iota(jnp.int32, (C, C), 0)
    ci = lax.broadcasted_iota(jnp.int32, (C, C), 1)
    tri = (ci <= ri).astype(BF16)
    x = ri ^ ci
    x = x | (x >> 1)
    x = x | (x >> 2)
    x = x | (x >> 4)
    hb = x - (x >> 1)
    rowi = lax.broadcasted_iota(jnp.int32, (C, 1), 0)
    lane = lax.broadcasted_iota(jnp.int32, (C, LANES), 1)
    head_lo = lane < GLA_DK
    sr = lax.broadcasted_iota(jnp.int32, (2 * GLA_DV, 2 * GLA_DK), 0)
    sc = lax.broadcasted_iota(jnp.int32, (2 * GLA_DV, 2 * GLA_DK), 1)
    sdiag = (sr < GLA_DV) == (sc < GLA_DK)

    q = q_ref[...].astype(F32)
    k = k_ref[...].astype(F32)
    v = v_ref[...]
    gf = gf_ref[...]
    gb = gb_ref[...]
    cf = _cumsum_rows(tri, gf)
    cb = _cumsum_rows(tri, gb) - gb

    a = [jnp.zeros((C, C), F32) for _ in range(GLA_HEADS)]
    w = C // 2
    while w >= 1:
        odd = (rowi & w) != 0
        d_f = cf - _pair_ref_rows(cf, w, rowi)
        d_b = cb - _pair_ref_rows(cb, w, rowi)
        qt = (q * jnp.exp(jnp.where(odd, d_f, -d_b))).astype(BF16)
        kt = (k * jnp.exp(jnp.where(odd, d_b, -d_f))).astype(BF16)
        sel = hb == w
        for h in range(GLA_HEADS):
            pr = slice((h // 2) * LANES, (h // 2 + 1) * LANES)
            keep = head_lo if h % 2 == 0 else jnp.logical_not(head_lo)
            lhs = jnp.where(keep, qt[:, pr], jnp.zeros((), BF16))
            a[h] = a[h] + jnp.where(sel, _dot_nt(lhs, kt[:, pr]), 0.0)
        w //= 2
    qh = q_ref[...]
    kh = k_ref[...]
    for h in range(GLA_HEADS):
        pr = slice((h // 2) * LANES, (h // 2 + 1) * LANES)
        keep = head_lo if h % 2 == 0 else jnp.logical_not(head_lo)
        lhs = jnp.where(keep, qh[:, pr], jnp.zeros((), BF16))
        a[h] = a[h] + jnp.where(hb == 0, 2.0 * _dot_nt(lhs, kh[:, pr]), 0.0)

    cf_last = cf[C - 1:C, :]
    qe = (q * jnp.exp(cf)).astype(BF16)
    ke = (k * jnp.exp(cf_last - cf)).astype(BF16)
    dec = jnp.exp(cf_last)
    outs = []
    for p in range(2):
        pr = slice(p * LANES, (p + 1) * LANES)
        vr = slice(p * 2 * GLA_DV, (p + 1) * 2 * GLA_DV)
        st = sf_ref[p]
        inter = _dot_nt(qe[:, pr], st.astype(BF16))
        for hh in range(2):
            h = 2 * p + hh
            vh = slice(h * GLA_DV, (h + 1) * GLA_DV)
            outs.append(_dot(a[h].astype(BF16), v[:, vh]) + inter[:, hh * GLA_DV:(hh + 1) * GLA_DV])
        upd = _dot_tn(v[:, vr], ke[:, pr])
        sf_ref[p] = dec[:, pr] * st + jnp.where(sdiag, upd, 0.0)
    om_ref[...] = jnp.concatenate(outs, axis=1)

    q2 = qb_ref[...].astype(F32)
    k2 = kb_ref[...].astype(F32)
    v2 = vb_ref[...]
    g2 = gbb_ref[...]
    c2 = _cumsum_rows(tri, g2)
    tot = c2[C - 1:C, :]
    cx = c2 - g2
    qe2 = (q2 * jnp.exp(tot - cx)).astype(BF16)
    ke2 = (k2 * jnp.exp(cx)).astype(BF16)
    dec2 = jnp.exp(tot)
    outs = []
    for p in range(2):
        pr = slice(p * LANES, (p + 1) * LANES)
        vr = slice(p * 2 * GLA_DV, (p + 1) * 2 * GLA_DV)
        st = sb_ref[p]
        outs.append(_dot_nt(qe2[:, pr], st.astype(BF16)))
        upd = _dot_tn(v2[:, vr], ke2[:, pr])
        sb_ref[p] = dec2[:, pr] * st + jnp.where(sdiag, upd, 0.0)
    ob_ref[...] = jnp.concatenate(outs, axis=1)


def _gla_call(gq, gk, gv, gf, gb, B, S):
    M = gq.shape[0]
    C = GLA_C
    n = S // C
    fw = lambda width: pl.BlockSpec((C, width), lambda b, i: (b * n + i, 0))
    bw = lambda width: pl.BlockSpec((C, width), lambda b, i: (b * n + n - 1 - i, 0))
    return pl.pallas_call(
        _gla_kernel, grid=(B, n),
        in_specs=[fw(GLA_QK), fw(GLA_QK), fw(GLA_WIDTH), fw(GLA_QK), fw(GLA_QK),
                  bw(GLA_QK), bw(GLA_QK), bw(GLA_WIDTH), bw(GLA_QK)],
        out_specs=[fw(GLA_WIDTH), bw(GLA_WIDTH)],
        out_shape=[jax.ShapeDtypeStruct((M, GLA_WIDTH), F32)] * 2,
        scratch_shapes=[pltpu.VMEM((2, 2 * GLA_DV, 2 * GLA_DK), F32)] * 2,
        compiler_params=pltpu.CompilerParams(dimension_semantics=("parallel", "arbitrary"),
                                             vmem_limit_bytes=VMEM_LIMIT),
        name="gla")(gq, gk, gv, gf, gb, gq, gk, gv, gb)


def _layer_norm(h, g, b):
    mu = jnp.mean(h, axis=-1, keepdims=True)
    d = h - mu
    var = jnp.mean(d * d, axis=-1, keepdims=True)
    return d * lax.rsqrt(var + LN_EPS) * g + b


def _merge_kernel(x_ref, ya_ref, om_ref, ob_ref, rs_ref, wg_ref, bg_ref, gn_ref,
                  wa_ref, wb_ref, wo_ref, lg_ref, lb_ref, o_ref, *, alpha):
    x = x_ref[...]
    gates = jax.nn.sigmoid(_dot(x.astype(BF16), wg_ref[...]) + bg_ref[...])
    o = om_ref[...] + ob_ref[...]
    gn = gn_ref[...]
    cols = []
    for h in range(GLA_HEADS):
        oh = o[:, h * GLA_DV:(h + 1) * GLA_DV]
        ms = jnp.mean(oh * oh, axis=-1, keepdims=True)
        cols.append(oh * lax.rsqrt(ms + RMS_EPS) * gn)
    yb = (jnp.concatenate(cols, axis=1) * rs_ref[...].astype(F32)).astype(BF16)
    pa = _dot(ya_ref[...], wa_ref[...])
    pb = _dot(yb, wb_ref[...])
    mixin = gates[:, :D_MODEL] * pa + gates[:, D_MODEL:] * pb
    mix = _dot(mixin.astype(BF16), wo_ref[...])
    o_ref[...] = _layer_norm(alpha * x + mix, lg_ref[...], lb_ref[...])


def _merge_call(x2, ya, om, ob, rs, w, alpha):
    M = x2.shape[0]
    tm = MERGE_TM
    full = lambda a: pl.BlockSpec(a.shape, lambda i: (0,) * a.ndim)
    tok = lambda n: pl.BlockSpec((tm, n), lambda i: (i, 0))
    ws = [w["wgate"], w["bgate"], w["gnorm"], w["wbra"], w["wbrb"], w["wout"], w["ln1g"], w["ln1b"]]
    return pl.pallas_call(
        functools.partial(_merge_kernel, alpha=alpha), grid=(M // tm,),
        in_specs=[tok(D_MODEL), tok(512), tok(512), tok(512), tok(512)] + [full(a) for a in ws],
        out_specs=tok(D_MODEL), out_shape=jax.ShapeDtypeStruct((M, D_MODEL), F32),
        compiler_params=pltpu.CompilerParams(dimension_semantics=("parallel",),
                                             vmem_limit_bytes=VMEM_LIMIT),
        name="merge")(x2, ya, om, ob, rs, *ws)


def _mlp_kernel(x_ref, w1_ref, w2_ref, lg_ref, lb_ref, o_ref, *, alpha):
    x = x_ref[...]
    h = jnp.maximum(_dot(x.astype(BF16), w1_ref[...]), 0.0)
    h = _dot((h * h).astype(BF16), w2_ref[...])
    o_ref[...] = _layer_norm(alpha * x + h, lg_ref[...], lb_ref[...])


def _mlp_call(x1, w, alpha):
    M = x1.shape[0]
    tm = MLP_TM
    once = lambda a: pl.BlockSpec(a.shape, lambda i: (0,) * a.ndim, pipeline_mode=pl.Buffered(1))
    tok = pl.BlockSpec((tm, D_MODEL), lambda i: (i, 0))
    ws = [w["w1"], w["w2"], w["ln2g"], w["ln2b"]]
    return pl.pallas_call(
        functools.partial(_mlp_kernel, alpha=alpha), grid=(M // tm,),
        in_specs=[tok] + [once(a) for a in ws],
        out_specs=tok, out_shape=jax.ShapeDtypeStruct((M, D_MODEL), F32),
        compiler_params=pltpu.CompilerParams(dimension_semantics=("parallel",),
                                             vmem_limit_bytes=VMEM_LIMIT),
        name="mlp")(x1, *ws)


def _rope_tables(S):
    half = ROT_DIM // 2
    inv = 1.0 / (ROPE_THETA ** (jnp.arange(0, ROT_DIM, 2, dtype=F32) / ROT_DIM))
    ang = jnp.arange(S, dtype=F32)[:, None] * inv[None, :]
    cos, sin = jnp.cos(ang), jnp.sin(ang)
    d = jnp.arange(LANES) % DA_DK
    idx = d % half
    cos_t = jnp.where(d[None, :] < ROT_DIM, cos[:, idx], 1.0)
    sin_a = jnp.where(d[None, :] < half, -sin[:, idx], 0.0)
    sin_b = jnp.where((d[None, :] >= half) & (d[None, :] < ROT_DIM), sin[:, idx], 0.0)
    return cos_t.astype(F32), sin_a.astype(F32), sin_b.astype(F32)


def _layer_weights(l, w_in, subln_g, w_dec_f, b_dec_f, w_dec_b, b_dec_b, gla_norm_g, w_br_a, w_br_b,
                   b_gate, w_out, ln1_g, ln1_b, w_mlp1, w_mlp2, ln2_g, ln2_b):
    offs = [0]
    for s in IN_SIZES:
        offs.append(offs[-1] + s)
    col = lambda i: w_in[l][:, offs[i]:offs[i + 1]]
    zeros = jnp.zeros((GLA_RANK, GLA_QK), F32)
    row = lambda a: a[l].reshape(1, -1).astype(F32)
    return {
        "wq": (col(0) * (DA_DK ** -0.5 * LOG2E)).astype(BF16), "wk": col(1).astype(BF16), "wv": col(2).astype(BF16),
        "wgq": (col(3) * GLA_DK ** -0.5).astype(BF16), "wgk": col(4).astype(BF16),
        "wgv": col(5).astype(BF16), "wgr": col(6).astype(BF16),
        "wz": jnp.concatenate([col(7), col(8)], axis=1).astype(BF16),
        "wdec": jnp.concatenate([jnp.concatenate([w_dec_f[l], zeros], axis=1),
                                 jnp.concatenate([zeros, w_dec_b[l]], axis=1)], axis=0).astype(F32),
        "bdec": jnp.concatenate([b_dec_f[l], b_dec_b[l]]).reshape(1, -1).astype(F32),
        "wgate": col(9).astype(BF16), "bgate": row(b_gate),
        "subg": subln_g[l].reshape(-1, 1).astype(F32), "gnorm": row(gla_norm_g),
        "wbra": w_br_a[l].astype(BF16), "wbrb": w_br_b[l].astype(BF16), "wout": w_out[l].astype(BF16),
        "ln1g": row(ln1_g), "ln1b": row(ln1_b),
        "w1": w_mlp1[l].astype(BF16), "w2": w_mlp2[l].astype(BF16),
        "ln2g": row(ln2_g), "ln2b": row(ln2_b),
    }


def kernel(x_prompt, x_sample, w_in, lam_q1, lam_k1, lam_q2, lam_k2, subln_g, w_dec_f, b_dec_f, w_dec_b, b_dec_b, gla_norm_g, w_br_a, w_br_b, b_gate, w_out, ln1_g, ln1_b, w_mlp1, w_mlp2, ln2_g, ln2_b):
    depth = w_in.shape[0]
    alpha = (2 * depth) ** 0.25
    layers = []
    for l in range(depth):
        w = _layer_weights(l, w_in, subln_g, w_dec_f, b_dec_f, w_dec_b, b_dec_b, gla_norm_g,
                           w_br_a, w_br_b, b_gate, w_out, ln1_g, ln1_b, w_mlp1, w_mlp2, ln2_g, ln2_b)
        lamv = jnp.stack([lam_q1[l], lam_k1[l], lam_q2[l], lam_k2[l]]).astype(F32)
        layers.append((w, lamv, 0.8 - 0.6 * math.exp(-0.3 * l)))

    def trunk(x):
        B, S, D = x.shape
        tabs = _rope_tables(S)
        x2 = x.reshape(B * S, D)
        for w, lamv, lam_init in layers:
            q, k, vt, gq, gk, gv, rs, gf, gb = _proj_call(x2, B, S, w, tabs)
            ya = _attn_call(q.reshape(B, S, 512), k.reshape(B, S, 512), vt, lamv, w["subg"], lam_init)
            om, ob = _gla_call(gq, gk, gv, gf, gb, B, S)
            x1 = _merge_call(x2, ya.reshape(B * S, 512), om, ob, rs, w, alpha)
            x2 = _mlp_call(x1, w, alpha)
        return x2.reshape(B, S, D)

    return (trunk(x_prompt), trunk(x_sample))
```

```python
import functools
import math

import jax
import jax.numpy as jnp
from jax import lax
from jax.experimental import pallas as pl
from jax.experimental.pallas import tpu as pltpu

F32 = jnp.float32
BF16 = jnp.bfloat16

D_MODEL = 1024
DA_HEADS = 4
DA_DK = 64
DA_DV = 128
DA_WIDTH = 512
ROT_DIM = 16
ROPE_THETA = 500000.0
GLA_HEADS = 4
GLA_DK = 64
GLA_DV = 128
GLA_QK = 256
GLA_WIDTH = 512
GLA_RANK = 16
GLA_TAU = 16.0
D_FF = 4096
LN_EPS = 1e-5
RMS_EPS = 1e-6
IN_SIZES = (512, 512, 512, 256, 256, 512, 512, 16, 16, 2048)

LANES = 128
VMEM_LIMIT = 56 * 1024 * 1024

PROJ_TM = 512
ATT_TQ = 128
ATT_TK = 1024
ATT_RC = 128
LOG2E = math.log2(math.e)
GLA_C = 128
MERGE_TM = 512
MLP_TM = 512


def _dot(a, b):
    return jnp.dot(a, b, preferred_element_type=F32)


def _dot_nt(a, b):
    return lax.dot_general(a, b, (((1,), (1,)), ((), ())), preferred_element_type=F32)


def _dot_tn(a, b):
    return lax.dot_general(a, b, (((0,), (0,)), ((), ())), preferred_element_type=F32)


def _proj_kernel(x_ref, wq_ref, wk_ref, wv_ref, wgq_ref, wgk_ref, wgv_ref, wgr_ref, wz_ref,
                 wdec_ref, bdec_ref, cos_ref, sa_ref, sb_ref,
                 q_ref, k_ref, vt_ref, gq_ref, gk_ref, gv_ref, rs_ref, gf_ref, gb_ref):
    x = x_ref[...].astype(BF16)
    cos = cos_ref[...]
    sin_a = sa_ref[...]
    sin_b = sb_ref[...]

    def rope(t):
        cols = []
        for j in range(t.shape[1] // LANES):
            blk = t[:, j * LANES:(j + 1) * LANES]
            cols.append(blk * cos + pltpu.roll(blk, LANES - ROT_DIM // 2, 1) * sin_a
                        + pltpu.roll(blk, ROT_DIM // 2, 1) * sin_b)
        return jnp.concatenate(cols, axis=1)

    q_ref[...] = rope(_dot(x, wq_ref[...])).astype(BF16)
    k_ref[...] = rope(_dot(x, wk_ref[...])).astype(BF16)
    vt_ref[...] = _dot(x, wv_ref[...]).T.astype(BF16)
    gq_ref[...] = _dot(x, wgq_ref[...]).astype(BF16)
    gk_ref[...] = _dot(x, wgk_ref[...]).astype(BF16)
    gv_ref[...] = _dot(x, wgv_ref[...]).astype(BF16)
    r = _dot(x, wgr_ref[...])
    rs_ref[...] = (r * jax.nn.sigmoid(r)).astype(BF16)
    z = _dot(x, wz_ref[...])
    zz = jnp.dot(z, wdec_ref[...], precision=lax.Precision.HIGHEST,
                 preferred_element_type=F32) + bdec_ref[...]
    g = -(jnp.maximum(-zz, 0.0) + jnp.log1p(jnp.exp(-jnp.abs(zz)))) / GLA_TAU
    gf_ref[...] = g[:, :GLA_QK]
    gb_ref[...] = g[:, GLA_QK:]


def _proj_call(x2, B, S, w, tabs):
    M = x2.shape[0]
    tm = PROJ_TM
    ns = S // tm
    full = lambda a: pl.BlockSpec(a.shape, lambda i: (0,) * a.ndim)
    tok = lambda n: pl.BlockSpec((tm, n), lambda i: (i, 0))
    tab = pl.BlockSpec((tm, LANES), lambda i: (i % ns, 0))
    ins = [x2, w["wq"], w["wk"], w["wv"], w["wgq"], w["wgk"], w["wgv"], w["wgr"], w["wz"],
           w["wdec"], w["bdec"], tabs[0], tabs[1], tabs[2]]
    in_specs = [tok(D_MODEL)] + [full(a) for a in ins[1:11]] + [tab, tab, tab]
    out_shape = [
        jax.ShapeDtypeStruct((M, 512), BF16),
        jax.ShapeDtypeStruct((M, 512), BF16),
        jax.ShapeDtypeStruct((B, 512, S), BF16),
        jax.ShapeDtypeStruct((M, GLA_QK), BF16),
        jax.ShapeDtypeStruct((M, GLA_QK), BF16),
        jax.ShapeDtypeStruct((M, GLA_WIDTH), BF16),
        jax.ShapeDtypeStruct((M, GLA_WIDTH), BF16),
        jax.ShapeDtypeStruct((M, GLA_QK), F32),
        jax.ShapeDtypeStruct((M, GLA_QK), F32),
    ]
    out_specs = [tok(512), tok(512),
                 pl.BlockSpec((None, 512, tm), lambda i: (i // ns, 0, i % ns)),
                 tok(GLA_QK), tok(GLA_QK), tok(GLA_WIDTH), tok(GLA_WIDTH), tok(GLA_QK), tok(GLA_QK)]
    return pl.pallas_call(
        _proj_kernel, grid=(M // tm,), in_specs=in_specs, out_specs=out_specs, out_shape=out_shape,
        compiler_params=pltpu.CompilerParams(dimension_semantics=("parallel",),
                                             vmem_limit_bytes=VMEM_LIMIT),
        name="proj")(*ins)


def _attn_kernel(lamv_ref, g_ref, q_ref, k_ref, vt_ref, o_ref, s0_ref, s1_ref, p0_ref, p1_ref,
                 *, seq, lam_init):
    tq, tk, rc = ATT_TQ, ATT_TK, ATT_RC
    ncol = 2 * tq
    nch = tk // rc
    n = seq // tk
    nq = seq // tq
    s_bufs = (s0_ref, s1_ref)
    p_bufs = (p0_ref, p1_ref)

    lv = lamv_ref[...]
    lam = (jnp.exp(jnp.sum(lv[0:1] * lv[1:2], keepdims=True))
           - jnp.exp(jnp.sum(lv[2:3] * lv[3:4], keepdims=True)) + lam_init)
    gain = g_ref[...] * (1.0 - lam_init)

    def make_qq(qi):
        qt = q_ref[pl.ds(pl.multiple_of(qi * tq, tq), tq), :].astype(F32).T
        row = lax.broadcasted_iota(jnp.int32, qt.shape, 0)
        return jnp.concatenate([jnp.where(row < DA_DK, qt, 0.0),
                                jnp.where(row >= DA_DK, qt, 0.0)], axis=1).astype(BF16)

    def qk_stage(qq, t):
        sc = _dot(k_ref[t * tk:(t + 1) * tk, :], qq)
        s_bufs[t % 2][...] = sc
        return jnp.max(sc.reshape(tk // 8, 8, ncol), axis=0)

    def softmax_stage(t, mx, m, l):
        m_new = jnp.maximum(m, jnp.max(mx, axis=0, keepdims=True))
        alpha = jnp.exp2(m - m_new)
        mb = jnp.broadcast_to(m_new, (8, ncol))[None]
        ls = None
        for c in range(nch):
            rows = slice(c * rc, (c + 1) * rc)
            p = jnp.exp2(s_bufs[t % 2][rows, :].reshape(rc // 8, 8, ncol) - mb)
            part = jnp.sum(p, axis=0)
            ls = part if ls is None else ls + part
            p_bufs[t % 2][rows, :] = p.reshape(rc, ncol).astype(BF16)
        return m_new, alpha * l + jnp.sum(ls, axis=0, keepdims=True), alpha

    def pv_stage(t, alpha, acc):
        return alpha * acc + _dot(vt_ref[:, t * tk:(t + 1) * tk], p_bufs[t % 2][...])

    def finalize(qi, l, acc):
        on = acc / l
        o = on[:, :tq] - lam * on[:, tq:]
        ms = jnp.mean(o * o, axis=0, keepdims=True)
        y = o * lax.rsqrt(ms + RMS_EPS) * gain
        o_ref[pl.ds(pl.multiple_of(qi * tq, tq), tq), :] = y.T.astype(BF16)

    m_init = jnp.full((1, ncol), -jnp.inf, F32)
    l_init = jnp.zeros((1, ncol), F32)
    acc_init = jnp.zeros((DA_DV, ncol), F32)

    def query_tile(qi, qq, state, first):
        mx, m, l, alpha, acc = state
        for t in range(n):
            mx_new = qk_stage(qq, t)
            if t == 0 and not first:
                acc = pv_stage(n - 2, alpha, acc)
                m, l, alpha = softmax_stage(n - 1, mx, m, l)
            elif t == 1:
                if not first:
                    acc = pv_stage(n - 1, alpha, acc)
                    finalize(qi - 1, l, acc)
                m, l, alpha = softmax_stage(0, mx, m_init, l_init)
                acc = acc_init
            elif t >= 2:
                acc = pv_stage(t - 2, alpha, acc)
                m, l, alpha = softmax_stage(t - 1, mx, m, l)
            mx = mx_new
        return mx, m, l, alpha, acc

    state = (jnp.zeros((8, ncol), F32), m_init, l_init, jnp.ones((1, ncol), F32), acc_init)
    state = query_tile(0, make_qq(0), state, True)

    def body(qi, carry):
        qq, state = carry
        qq_next = make_qq(jnp.minimum(qi + 1, nq - 1))
        return qq_next, query_tile(qi, qq, state, False)

    _, state = lax.fori_loop(1, nq, body, (make_qq(1), state))
    mx, m, l, alpha, acc = state
    acc = pv_stage(n - 2, alpha, acc)
    m, l, alpha = softmax_stage(n - 1, mx, m, l)
    acc = pv_stage(n - 1, alpha, acc)
    finalize(nq - 1, l, acc)


def _attn_call(q, k, vt, lamv, subg, lam_init):
    B, S, _ = q.shape
    seq_blk = pl.BlockSpec((None, S, LANES), lambda b, h: (b, 0, h))
    return pl.pallas_call(
        functools.partial(_attn_kernel, seq=S, lam_init=lam_init),
        grid=(B, DA_HEADS),
        in_specs=[
            pl.BlockSpec(lamv.shape, lambda b, h: (0, 0)),
            pl.BlockSpec(subg.shape, lambda b, h: (0, 0)),
            seq_blk, seq_blk,
            pl.BlockSpec((None, LANES, S), lambda b, h: (b, h, 0)),
        ],
        out_specs=seq_blk,
        out_shape=jax.ShapeDtypeStruct((B, S, DA_WIDTH), BF16),
        scratch_shapes=[pltpu.VMEM((ATT_TK, 2 * ATT_TQ), F32)] * 2
        + [pltpu.VMEM((ATT_TK, 2 * ATT_TQ), BF16)] * 2,
        compiler_params=pltpu.CompilerParams(
            dimension_semantics=("parallel", "parallel"),
            vmem_limit_bytes=VMEM_LIMIT),
        name="diff_attn")(lamv, subg, q, k, vt)


def _cumsum_rows(tri, g):
    g1 = g.astype(BF16)
    r1 = g - g1.astype(F32)
    g2 = r1.astype(BF16)
    g3 = (r1 - g2.astype(F32)).astype(BF16)
    return _dot(tri, g1) + _dot(tri, g2) + _dot(tri, g3)


def _pair_ref_rows(c, w, rowi):
    C, n = c.shape
    if w >= 8:
        c3 = c.reshape(C // (2 * w), 2 * w, n)
        return jnp.broadcast_to(c3[:, w:w + 1, :], c3.shape).reshape(C, n)
    rem = rowi & (2 * w - 1)
    out = c
    for r in range(2 * w):
        if r == w:
            continue
        out = jnp.where(rem == r, pltpu.roll(c, (r - w) % C, 0), out)
    return out


def _gla_kernel(q_ref, k_ref, v_ref, gf_ref, gb_ref, qb_ref, kb_ref, vb_ref, gbb_ref,
                om_ref, ob_ref, sf_ref, sb_ref):
    C = GLA_C
    step = pl.program_id(1)

    @pl.when(step == 0)
    def _():
        sf_ref[...] = jnp.zeros_like(sf_ref)
        sb_ref[...] = jnp.zeros_like(sb_ref)

    ri = lax.broadcasted_iota(jnp.int32, (C, C), 0)
    ci = lax.broadcasted_iota(jnp.int32, (C, C), 1)
    tri = (ci <= ri).astype(BF16)
    x = ri ^ ci
    x = x | (x >> 1)
    x = x | (x >> 2)
    x = x | (x >> 4)
    hb = x - (x >> 1)
    rowi = lax.broadcasted_iota(jnp.int32, (C, 1), 0)
    lane = lax.broadcasted_iota(jnp.int32, (C, LANES), 1)
    head_lo = lane < GLA_DK
    sr = lax.broadcasted_iota(jnp.int32, (2 * GLA_DV, 2 * GLA_DK), 0)
    sc = lax.broadcasted_iota(jnp.int32, (2 * GLA_DV, 2 * GLA_DK), 1)
    sdiag = (sr < GLA_DV) == (sc < GLA_DK)

    q = q_ref[...].astype(F32)
    k = k_ref[...].astype(F32)
    v = v_ref[...]
    gf = gf_ref[...]
    gb = gb_ref[...]
    cf = _cumsum_rows(tri, gf)
    cb = _cumsum_rows(tri, gb) - gb

    a = [jnp.zeros((C, C), F32) for _ in range(GLA_HEADS)]
    w = C // 2
    while w >= 1:
        odd = (rowi & w) != 0
        d_f = cf - _pair_ref_rows(cf, w, rowi)
        d_b = cb - _pair_ref_rows(cb, w, rowi)
        qt = (q * jnp.exp(jnp.where(odd, d_f, -d_b))).astype(BF16)
        kt = (k * jnp.exp(jnp.where(odd, d_b, -d_f))).astype(BF16)
        sel = hb == w
        for h in range(GLA_HEADS):
            pr = slice((h // 2) * LANES, (h // 2 + 1) * LANES)
            keep = head_lo if h % 2 == 0 else jnp.logical_not(head_lo)
            lhs = jnp.where(keep, qt[:, pr], jnp.zeros((), BF16))
            a[h] = a[h] + jnp.where(sel, _dot_nt(lhs, kt[:, pr]), 0.0)
        w //= 2
    qh = q_ref[...]
    kh = k_ref[...]
    for h in range(GLA_HEADS):
        pr = slice((h // 2) * LANES, (h // 2 + 1) * LANES)
        keep = head_lo if h % 2 == 0 else jnp.logical_not(head_lo)
        lhs = jnp.where(keep, qh[:, pr], jnp.zeros((), BF16))
        a[h] = a[h] + jnp.where(hb == 0, 2.0 * _dot_nt(lhs, kh[:, pr]), 0.0)

    cf_last = cf[C - 1:C, :]
    qe = (q * jnp.exp(cf)).astype(BF16)
    ke = (k * jnp.exp(cf_last - cf)).astype(BF16)
    dec = jnp.exp(cf_last)
    outs = []
    for p in range(2):
        pr = slice(p * LANES, (p + 1) * LANES)
        vr = slice(p * 2 * GLA_DV, (p + 1) * 2 * GLA_DV)
        st = sf_ref[p]
        inter = _dot_nt(qe[:, pr], st.astype(BF16))
        for hh in range(2):
            h = 2 * p + hh
            vh = slice(h * GLA_DV, (h + 1) * GLA_DV)
            outs.append(_dot(a[h].astype(BF16), v[:, vh]) + inter[:, hh * GLA_DV:(hh + 1) * GLA_DV])
        upd = _dot_tn(v[:, vr], ke[:, pr])
        sf_ref[p] = dec[:, pr] * st + jnp.where(sdiag, upd, 0.0)
    om_ref[...] = jnp.concatenate(outs, axis=1)

    q2 = qb_ref[...].astype(F32)
    k2 = kb_ref[...].astype(F32)
    v2 = vb_ref[...]
    g2 = gbb_ref[...]
    c2 = _cumsum_rows(tri, g2)
    tot = c2[C - 1:C, :]
    cx = c2 - g2
    qe2 = (q2 * jnp.exp(tot - cx)).astype(BF16)
    ke2 = (k2 * jnp.exp(cx)).astype(BF16)
    dec2 = jnp.exp(tot)
    outs = []
    for p in range(2):
        pr = slice(p * LANES, (p + 1) * LANES)
        vr = slice(p * 2 * GLA_DV, (p + 1) * 2 * GLA_DV)
        st = sb_ref[p]
        outs.append(_dot_nt(qe2[:, pr], st.astype(BF16)))
        upd = _dot_tn(v2[:, vr], ke2[:, pr])
        sb_ref[p] = dec2[:, pr] * st + jnp.where(sdiag, upd, 0.0)
    ob_ref[...] = jnp.concatenate(outs, axis=1)


def _gla_call(gq, gk, gv, gf, gb, B, S):
    M = gq.shape[0]
    C = GLA_C
    n = S // C
    fw = lambda width: pl.BlockSpec((C, width), lambda b, i: (b * n + i, 0))
    bw = lambda width: pl.BlockSpec((C, width), lambda b, i: (b * n + n - 1 - i, 0))
    return pl.pallas_call(
        _gla_kernel, grid=(B, n),
        in_specs=[fw(GLA_QK), fw(GLA_QK), fw(GLA_WIDTH), fw(GLA_QK), fw(GLA_QK),
                  bw(GLA_QK), bw(GLA_QK), bw(GLA_WIDTH), bw(GLA_QK)],
        out_specs=[fw(GLA_WIDTH), bw(GLA_WIDTH)],
        out_shape=[jax.ShapeDtypeStruct((M, GLA_WIDTH), F32)] * 2,
        scratch_shapes=[pltpu.VMEM((2, 2 * GLA_DV, 2 * GLA_DK), F32)] * 2,
        compiler_params=pltpu.CompilerParams(dimension_semantics=("parallel", "arbitrary"),
                                             vmem_limit_bytes=VMEM_LIMIT),
        name="gla")(gq, gk, gv, gf, gb, gq, gk, gv, gb)


def _layer_norm(h, g, b):
    mu = jnp.mean(h, axis=-1, keepdims=True)
    d = h - mu
    var = jnp.mean(d * d, axis=-1, keepdims=True)
    return d * lax.rsqrt(var + LN_EPS) * g + b


def _merge_kernel(x_ref, ya_ref, om_ref, ob_ref, rs_ref, wg_ref, bg_ref, gn_ref,
                  wa_ref, wb_ref, wo_ref, lg_ref, lb_ref, o_ref, *, alpha):
    x = x_ref[...]
    gates = jax.nn.sigmoid(_dot(x.astype(BF16), wg_ref[...]) + bg_ref[...])
    o = om_ref[...] + ob_ref[...]
    gn = gn_ref[...]
    cols = []
    for h in range(GLA_HEADS):
        oh = o[:, h * GLA_DV:(h + 1) * GLA_DV]
        ms = jnp.mean(oh * oh, axis=-1, keepdims=True)
        cols.append(oh * lax.rsqrt(ms + RMS_EPS) * gn)
    yb = (jnp.concatenate(cols, axis=1) * rs_ref[...].astype(F32)).astype(BF16)
    pa = _dot(ya_ref[...], wa_ref[...])
    pb = _dot(yb, wb_ref[...])
    mixin = gates[:, :D_MODEL] * pa + gates[:, D_MODEL:] * pb
    mix = _dot(mixin.astype(BF16), wo_ref[...])
    o_ref[...] = _layer_norm(alpha * x + mix, lg_ref[...], lb_ref[...])


def _merge_call(x2, ya, om, ob, rs, w, alpha):
    M = x2.shape[0]
    tm = MERGE_TM
    full = lambda a: pl.BlockSpec(a.shape, lambda i: (0,) * a.ndim)
    tok = lambda n: pl.BlockSpec((tm, n), lambda i: (i, 0))
    ws = [w["wgate"], w["bgate"], w["gnorm"], w["wbra"], w["wbrb"], w["wout"], w["ln1g"], w["ln1b"]]
    return pl.pallas_call(
        functools.partial(_merge_kernel, alpha=alpha), grid=(M // tm,),
        in_specs=[tok(D_MODEL), tok(512), tok(512), tok(512), tok(512)] + [full(a) for a in ws],
        out_specs=tok(D_MODEL), out_shape=jax.ShapeDtypeStruct((M, D_MODEL), F32),
        compiler_params=pltpu.CompilerParams(dimension_semantics=("parallel",),
                                             vmem_limit_bytes=VMEM_LIMIT),
        name="merge")(x2, ya, om, ob, rs, *ws)


def _mlp_kernel(x_ref, w1_ref, w2_ref, lg_ref, lb_ref, o_ref, *, alpha):
    x = x_ref[...]
    h = jnp.maximum(_dot(x.astype(BF16), w1_ref[...]), 0.0)
    h = _dot((h * h).astype(BF16), w2_ref[...])
    o_ref[...] = _layer_norm(alpha * x + h, lg_ref[...], lb_ref[...])


def _mlp_call(x1, w, alpha):
    M = x1.shape[0]
    tm = MLP_TM
    once = lambda a: pl.BlockSpec(a.shape, lambda i: (0,) * a.ndim, pipeline_mode=pl.Buffered(1))
    tok = pl.BlockSpec((tm, D_MODEL), lambda i: (i, 0))
    ws = [w["w1"], w["w2"], w["ln2g"], w["ln2b"]]
    return pl.pallas_call(
        functools.partial(_mlp_kernel, alpha=alpha), grid=(M // tm,),
        in_specs=[tok] + [once(a) for a in ws],
        out_specs=tok, out_shape=jax.ShapeDtypeStruct((M, D_MODEL), F32),
        compiler_params=pltpu.CompilerParams(dimension_semantics=("parallel",),
                                             vmem_limit_bytes=VMEM_LIMIT),
        name="mlp")(x1, *ws)


def _rope_tables(S):
    half = ROT_DIM // 2
    inv = 1.0 / (ROPE_THETA ** (jnp.arange(0, ROT_DIM, 2, dtype=F32) / ROT_DIM))
    ang = jnp.arange(S, dtype=F32)[:, None] * inv[None, :]
    cos, sin = jnp.cos(ang), jnp.sin(ang)
    d = jnp.arange(LANES) % DA_DK
    idx = d % half
    cos_t = jnp.where(d[None, :] < ROT_DIM, cos[:, idx], 1.0)
    sin_a = jnp.where(d[None, :] < half, -sin[:, idx], 0.0)
    sin_b = jnp.where((d[None, :] >= half) & (d[None, :] < ROT_DIM), sin[:, idx], 0.0)
    return cos_t.astype(F32), sin_a.astype(F32), sin_b.astype(F32)


def _layer_weights(l, w_in, subln_g, w_dec_f, b_dec_f, w_dec_b, b_dec_b, gla_norm_g, w_br_a, w_br_b,
                   b_gate, w_out, ln1_g, ln1_b, w_mlp1, w_mlp2, ln2_g, ln2_b):
    offs = [0]
    for s in IN_SIZES:
        offs.append(offs[-1] + s)
    col = lambda i: w_in[l][:, offs[i]:offs[i + 1]]
    zeros = jnp.zeros((GLA_RANK, GLA_QK), F32)
    row = lambda a: a[l].reshape(1, -1).astype(F32)
    return {
        "wq": (col(0) * (DA_DK ** -0.5 * LOG2E)).astype(BF16), "wk": col(1).astype(BF16), "wv": col(2).astype(BF16),
        "wgq": (col(3) * GLA_DK ** -0.5).astype(BF16), "wgk": col(4).astype(BF16),
        "wgv": col(5).astype(BF16), "wgr": col(6).astype(BF16),
        "wz": jnp.concatenate([col(7), col(8)], axis=1).astype(BF16),
        "wdec": jnp.concatenate([jnp.concatenate([w_dec_f[l], zeros], axis=1),
                                 jnp.concatenate([zeros, w_dec_b[l]], axis=1)], axis=0).astype(F32),
        "bdec": jnp.concatenate([b_dec_f[l], b_dec_b[l]]).reshape(1, -1).astype(F32),
        "wgate": col(9).astype(BF16), "bgate": row(b_gate),
        "subg": subln_g[l].reshape(-1, 1).astype(F32), "gnorm": row(gla_norm_g),
        "wbra": w_br_a[l].astype(BF16), "wbrb": w_br_b[l].astype(BF16), "wout": w_out[l].astype(BF16),
        "ln1g": row(ln1_g), "ln1b": row(ln1_b),
        "w1": w_mlp1[l].astype(BF16), "w2": w_mlp2[l].astype(BF16),
        "ln2g": row(ln2_g), "ln2b": row(ln2_b),
    }


def kernel(x_prompt, x_sample, w_in, lam_q1, lam_k1, lam_q2, lam_k2, subln_g, w_dec_f, b_dec_f, w_dec_b, b_dec_b, gla_norm_g, w_br_a, w_br_b, b_gate, w_out, ln1_g, ln1_b, w_mlp1, w_mlp2, ln2_g, ln2_b):
    depth = w_in.shape[0]
    alpha = (2 * depth) ** 0.25
    layers = []
    for l in range(depth):
        w = _layer_weights(l, w_in, subln_g, w_dec_f, b_dec_f, w_dec_b, b_dec_b, gla_norm_g,
                           w_br_a, w_br_b, b_gate, w_out, ln1_g, ln1_b, w_mlp1, w_mlp2, ln2_g, ln2_b)
        lamv = jnp.stack([lam_q1[l], lam_k1[l], lam_q2[l], lam_k2[l]]).astype(F32)
        layers.append((w, lamv, 0.8 - 0.6 * math.exp(-0.3 * l)))

    def trunk(x):
        B, S, D = x.shape
        tabs = _rope_tables(S)
        x2 = x.reshape(B * S, D)
        for w, lamv, lam_init in layers:
            q, k, vt, gq, gk, gv, rs, gf, gb = _proj_call(x2, B, S, w, tabs)
            ya = _attn_call(q.reshape(B, S, 512), k.reshape(B, S, 512), vt, lamv, w["subg"], lam_init)
            om, ob = _gla_call(gq, gk, gv, gf, gb, B, S)
            x1 = _merge_call(x2, ya.reshape(B * S, 512), om, ob, rs, w, alpha)
            x2 = _mlp_call(x1, w, alpha)
        return x2.reshape(B, S, D)

    return (trunk(x_prompt), trunk(x_sample))
```

```python
import functools
import math

import jax
import jax.numpy as jnp
import numpy as np
from jax import lax
from jax.experimental import pallas as pl
from jax.experimental.pallas import tpu as pltpu

F32 = jnp.float32
BF16 = jnp.bfloat16

D_MODEL = 1024
DA_HEADS = 4
DA_DK = 64
DA_DV = 128
DA_WIDTH = 512
ROT_DIM = 16
ROPE_THETA = 500000.0
GLA_HEADS = 4
GLA_DK = 64
GLA_DV = 128
GLA_QK = 256
GLA_WIDTH = 512
GLA_RANK = 16
GLA_TAU = 16.0
D_FF = 4096
LN_EPS = 1e-5
RMS_EPS = 1e-6
IN_SIZES = (512, 512, 512, 256, 256, 512, 512, 16, 16, 2048)

LANES = 128
VMEM_LIMIT = 56 * 1024 * 1024

PROJ_TM = 512
ATT_TQ = 128
ATT_TK = 1024
ATT_RC = 128
LOG2E = math.log2(math.e)
GLA_C = 128
GLA_SUB = 4
MERGE_TM = 512
MLP_TM = 512


def _dot(a, b):
    return jnp.dot(a, b, preferred_element_type=F32)


def _dot_nt(a, b):
    return lax.dot_general(a, b, (((1,), (1,)), ((), ())), preferred_element_type=F32)


def _dot_tn(a, b):
    return lax.dot_general(a, b, (((0,), (0,)), ((), ())), preferred_element_type=F32)


def _proj_kernel(x_ref, wq_ref, wk_ref, wv_ref, wgq_ref, wgk_ref, wgv_ref, wgr_ref, wz_ref,
                 wdec_ref, bdec_ref, cos_ref, sa_ref, sb_ref,
                 q_ref, k_ref, vt_ref, gq_ref, gk_ref, gv_ref, rs_ref, gf_ref, gb_ref):
    x = x_ref[...].astype(BF16)
    cos = cos_ref[...]
    sin_a = sa_ref[...]
    sin_b = sb_ref[...]

    def rope(t):
        cols = []
        for j in range(t.shape[1] // LANES):
            blk = t[:, j * LANES:(j + 1) * LANES]
            cols.append(blk * cos + pltpu.roll(blk, LANES - ROT_DIM // 2, 1) * sin_a
                        + pltpu.roll(blk, ROT_DIM // 2, 1) * sin_b)
        return jnp.concatenate(cols, axis=1)

    q_ref[...] = rope(_dot(x, wq_ref[...])).astype(BF16)
    k_ref[...] = rope(_dot(x, wk_ref[...])).astype(BF16)
    vt_ref[...] = _dot(x, wv_ref[...]).T.astype(BF16)
    gq_ref[...] = _dot(x, wgq_ref[...]).astype(BF16)
    gk_ref[...] = _dot(x, wgk_ref[...]).astype(BF16)
    gv_ref[...] = _dot(x, wgv_ref[...]).astype(BF16)
    r = _dot(x, wgr_ref[...])
    rs_ref[...] = (r * jax.nn.sigmoid(r)).astype(BF16)
    z = _dot(x, wz_ref[...])
    zz = _dot(z.astype(BF16), wdec_ref[...]) + bdec_ref[...]
    g = -(jnp.maximum(-zz, 0.0) + jnp.log1p(jnp.exp(-jnp.abs(zz)))) / GLA_TAU
    gf_ref[...] = g[:, :GLA_QK]
    gb_ref[...] = g[:, GLA_QK:]


def _proj_call(x2, B, S, w, tabs):
    M = x2.shape[0]
    tm = PROJ_TM
    ns = S // tm
    full = lambda a: pl.BlockSpec(a.shape, lambda i: (0,) * a.ndim)
    tok = lambda n: pl.BlockSpec((tm, n), lambda i: (i, 0))
    tab = pl.BlockSpec((tm, LANES), lambda i: (i % ns, 0))
    ins = [x2, w["wq"], w["wk"], w["wv"], w["wgq"], w["wgk"], w["wgv"], w["wgr"], w["wz"],
           w["wdec"], w["bdec"], tabs[0], tabs[1], tabs[2]]
    in_specs = [tok(D_MODEL)] + [full(a) for a in ins[1:11]] + [tab, tab, tab]
    out_shape = [
        jax.ShapeDtypeStruct((M, 512), BF16),
        jax.ShapeDtypeStruct((M, 512), BF16),
        jax.ShapeDtypeStruct((B, 512, S), BF16),
        jax.ShapeDtypeStruct((M, GLA_QK), BF16),
        jax.ShapeDtypeStruct((M, GLA_QK), BF16),
        jax.ShapeDtypeStruct((M, GLA_WIDTH), BF16),
        jax.ShapeDtypeStruct((M, GLA_WIDTH), BF16),
        jax.ShapeDtypeStruct((M, GLA_QK), F32),
        jax.ShapeDtypeStruct((M, GLA_QK), F32),
    ]
    out_specs = [tok(512), tok(512),
                 pl.BlockSpec((None, 512, tm), lambda i: (i // ns, 0, i % ns)),
                 tok(GLA_QK), tok(GLA_QK), tok(GLA_WIDTH), tok(GLA_WIDTH), tok(GLA_QK), tok(GLA_QK)]
    return pl.pallas_call(
        _proj_kernel, grid=(M // tm,), in_specs=in_specs, out_specs=out_specs, out_shape=out_shape,
        compiler_params=pltpu.CompilerParams(dimension_semantics=("parallel",),
                                             vmem_limit_bytes=VMEM_LIMIT),
        name="proj")(*ins)


def _attn_kernel(lamv_ref, g_ref, q_ref, k_ref, vt_ref, o_ref, s0_ref, s1_ref, p0_ref, p1_ref,
                 *, seq, lam_init):
    tq, tk, rc = ATT_TQ, ATT_TK, ATT_RC
    ncol = 2 * tq
    nch = tk // rc
    n = seq // tk
    nq = seq // tq
    s_bufs = (s0_ref, s1_ref)
    p_bufs = (p0_ref, p1_ref)

    lv = lamv_ref[...]
    lam = (jnp.exp(jnp.sum(lv[0:1] * lv[1:2], keepdims=True))
           - jnp.exp(jnp.sum(lv[2:3] * lv[3:4], keepdims=True)) + lam_init)
    gain = g_ref[...] * (1.0 - lam_init)

    def make_qq(qi):
        qt = q_ref[pl.ds(pl.multiple_of(qi * tq, tq), tq), :].astype(F32).T
        row = lax.broadcasted_iota(jnp.int32, qt.shape, 0)
        return jnp.concatenate([jnp.where(row < DA_DK, qt, 0.0),
                                jnp.where(row >= DA_DK, qt, 0.0)], axis=1).astype(BF16)

    def qk_stage(qq, t):
        sc = _dot(k_ref[t * tk:(t + 1) * tk, :], qq)
        s_bufs[t % 2][...] = sc
        return jnp.max(sc.reshape(tk // 8, 8, ncol), axis=0)

    def softmax_stage(t, mx, m, l):
        m_new = jnp.maximum(m, jnp.max(mx, axis=0, keepdims=True))
        alpha = jnp.exp2(m - m_new)
        mb = jnp.broadcast_to(m_new, (8, ncol))[None]
        ls = None
        for c in range(nch):
            rows = slice(c * rc, (c + 1) * rc)
            p = jnp.exp2(s_bufs[t % 2][rows, :].reshape(rc // 8, 8, ncol) - mb)
            part = jnp.sum(p, axis=0)
            ls = part if ls is None else ls + part
            p_bufs[t % 2][rows, :] = p.reshape(rc, ncol).astype(BF16)
        return m_new, alpha * l + jnp.sum(ls, axis=0, keepdims=True), alpha

    def pv_stage(t, alpha, acc):
        return alpha * acc + _dot(vt_ref[:, t * tk:(t + 1) * tk], p_bufs[t % 2][...])

    def finalize(qi, l, acc):
        on = acc / l
        o = on[:, :tq] - lam * on[:, tq:]
        ms = jnp.mean(o * o, axis=0, keepdims=True)
        y = o * lax.rsqrt(ms + RMS_EPS) * gain
        o_ref[pl.ds(pl.multiple_of(qi * tq, tq), tq), :] = y.T.astype(BF16)

    m_init = jnp.full((1, ncol), -jnp.inf, F32)
    l_init = jnp.zeros((1, ncol), F32)
    acc_init = jnp.zeros((DA_DV, ncol), F32)

    def query_tile(qi, qq, state, first):
        mx, m, l, alpha, acc = state
        for t in range(n):
            mx_new = qk_stage(qq, t)
            if t == 0 and not first:
                acc = pv_stage(n - 2, alpha, acc)
                m, l, alpha = softmax_stage(n - 1, mx, m, l)
            elif t == 1:
                if not first:
                    acc = pv_stage(n - 1, alpha, acc)
                    finalize(qi - 1, l, acc)
                m, l, alpha = softmax_stage(0, mx, m_init, l_init)
                acc = acc_init
            elif t >= 2:
                acc = pv_stage(t - 2, alpha, acc)
                m, l, alpha = softmax_stage(t - 1, mx, m, l)
            mx = mx_new
        return mx, m, l, alpha, acc

    state = (jnp.zeros((8, ncol), F32), m_init, l_init, jnp.ones((1, ncol), F32), acc_init)
    state = query_tile(0, make_qq(0), state, True)

    def body(qi, carry):
        qq, state = carry
        qq_next = make_qq(jnp.minimum(qi + 1, nq - 1))
        return qq_next, query_tile(qi, qq, state, False)

    _, state = lax.fori_loop(1, nq, body, (make_qq(1), state))
    mx, m, l, alpha, acc = state
    acc = pv_stage(n - 2, alpha, acc)
    m, l, alpha = softmax_stage(n - 1, mx, m, l)
    acc = pv_stage(n - 1, alpha, acc)
    finalize(nq - 1, l, acc)


def _attn_call(q, k, vt, lamv, subg, lam_init):
    B, S, _ = q.shape
    seq_blk = pl.BlockSpec((None, S, LANES), lambda b, h: (b, 0, h))
    return pl.pallas_call(
        functools.partial(_attn_kernel, seq=S, lam_init=lam_init),
        grid=(B, DA_HEADS),
        in_specs=[
            pl.BlockSpec(lamv.shape, lambda b, h: (0, 0)),
            pl.BlockSpec(subg.shape, lambda b, h: (0, 0)),
            seq_blk, seq_blk,
            pl.BlockSpec((None, LANES, S), lambda b, h: (b, h, 0)),
        ],
        out_specs=seq_blk,
        out_shape=jax.ShapeDtypeStruct((B, S, DA_WIDTH), BF16),
        scratch_shapes=[pltpu.VMEM((ATT_TK, 2 * ATT_TQ), F32)] * 2
        + [pltpu.VMEM((ATT_TK, 2 * ATT_TQ), BF16)] * 2,
        compiler_params=pltpu.CompilerParams(
            dimension_semantics=("parallel", "parallel"),
            vmem_limit_bytes=VMEM_LIMIT),
        name="diff_attn")(lamv, subg, q, k, vt)


GLA_MXU_LEVELS = (4, 2, 1)


def _split2(g):
    g1 = g.astype(BF16)
    return g1, (g - g1.astype(F32)).astype(BF16)


def _gla_tables(C):
    t = np.arange(C)
    tri_i = (t[None, :] <= t[:, None]).astype(np.float32)
    tri_s = (t[None, :] < t[:, None]).astype(np.float32)
    blocks = []
    for w in GLA_MXU_LEVELS:
        rho = (t // (2 * w)) * 2 * w + w
        odd = ((t & w) != 0).astype(np.float32)[:, None]
        d_f = tri_i - tri_i[rho]
        d_b = tri_s - tri_s[rho]
        blocks.append(np.concatenate([odd * d_f, (odd - 1.0) * d_b], 1))
        blocks.append(np.concatenate([(odd - 1.0) * d_f, odd * d_b], 1))
    tab = np.concatenate(blocks, 0)
    x = t[:, None] ^ t[None, :]
    hb = np.where(x > 0, 2 ** np.floor(np.log2(np.maximum(x, 1))), 0).astype(np.int32)
    tri = np.concatenate([tri_i, tri_s], 0)
    return (jnp.asarray(np.concatenate([tab, tab], 1), BF16),
            jnp.asarray(np.concatenate([tri, tri], 1), BF16), jnp.asarray(hb))


def _pair_ref_rows(c, w):
    C, n = c.shape
    c3 = c.reshape(C // (2 * w), 2 * w, n)
    return jnp.broadcast_to(c3[:, w:w + 1, :], c3.shape).reshape(C, n)


def _gla_kernel(tab_ref, tri_ref, hb_ref, q_ref, k_ref, v_ref, gf_ref, gb_ref,
                qb_ref, kb_ref, vb_ref, gbb_ref, om_ref, ob_ref, sf_ref, sb_ref):
    @pl.when(pl.program_id(1) == 0)
    def _():
        sf_ref[...] = jnp.zeros_like(sf_ref)
        sb_ref[...] = jnp.zeros_like(sb_ref)

    chunks = []
    for j in range(GLA_SUB):
        rf = pl.ds(j * GLA_C, GLA_C)
        rb = pl.ds((GLA_SUB - 1 - j) * GLA_C, GLA_C)
        chunks.append(_gla_chunk(tab_ref, tri_ref, hb_ref, q_ref.at[rf], k_ref.at[rf], v_ref.at[rf],
                                 gf_ref.at[rf], gb_ref.at[rf], qb_ref.at[rb], kb_ref.at[rb],
                                 vb_ref.at[rb], gbb_ref.at[rb], om_ref.at[rf], ob_ref.at[rb],
                                 sf_ref, sb_ref))
    for _ in range(GLA_PHASES):
        for chunk in chunks:
            next(chunk)


GLA_PHASES = 3


def _gla_chunk(tab_ref, tri_ref, hb_ref, q_ref, k_ref, v_ref, gf_ref, gb_ref,
               qb_ref, kb_ref, vb_ref, gbb_ref, om_ref, ob_ref, sf_ref, sb_ref):
    C = GLA_C
    hb = hb_ref[...]
    rowi = lax.broadcasted_iota(jnp.int32, (C, 1), 0)
    lane = lax.broadcasted_iota(jnp.int32, (C, LANES), 1)
    head_lo = lane < GLA_DK
    sr = lax.broadcasted_iota(jnp.int32, (2 * GLA_DV, 2 * GLA_DK), 0)
    sc = lax.broadcasted_iota(jnp.int32, (2 * GLA_DV, 2 * GLA_DK), 1)
    sdiag = (sr < GLA_DV) == (sc < GLA_DK)

    q2 = qb_ref[...].astype(F32)
    k2 = kb_ref[...].astype(F32)
    v2 = vb_ref[...]
    g2 = gbb_ref[...]
    cx = _dot(tri_ref[C:2 * C, :], jnp.concatenate(_split2(g2), axis=0))
    tot = cx[C - 1:C, :] + g2[C - 1:C, :]
    qe2 = (q2 * jnp.exp(tot - cx)).astype(BF16)
    ke2 = (k2 * jnp.exp(cx)).astype(BF16)
    dec2 = jnp.exp(tot)

    q = q_ref[...].astype(F32)
    k = k_ref[...].astype(F32)
    v = v_ref[...]
    gf1, gf2 = _split2(gf_ref[...])
    gb1, gb2 = _split2(gb_ref[...])
    cf = _dot(tri_ref[0:C, :], jnp.concatenate([gf1, gf2], axis=0))
    cb = _dot(tri_ref[C:2 * C, :], jnp.concatenate([gb1, gb2], axis=0))
    g4 = jnp.concatenate([gf1, gb1, gf2, gb2], axis=0)

    levels = [C >> (i + 1) for i in range(C.bit_length() - 1)]
    operands = []
    for w in levels:
        if w in GLA_MXU_LEVELS:
            base = 2 * C * GLA_MXU_LEVELS.index(w)
            arg = _dot(tab_ref[base:base + 2 * C, :], g4)
            arg_q, arg_k = arg[:C], arg[C:]
        else:
            odd = (rowi & w) != 0
            d_f = cf - _pair_ref_rows(cf, w)
            d_b = cb - _pair_ref_rows(cb, w)
            arg_q = jnp.where(odd, d_f, -d_b)
            arg_k = jnp.where(odd, d_b, -d_f)
        operands.append(((q * jnp.exp(arg_q)).astype(BF16), (k * jnp.exp(arg_k)).astype(BF16)))
    operands.append((q_ref[...], k_ref[...]))
    cf_last = cf[C - 1:C, :]
    qe = (q * jnp.exp(cf)).astype(BF16)
    ke = (k * jnp.exp(cf_last - cf)).astype(BF16)
    dec = jnp.exp(cf_last)
    yield

    def head_blocks(qt, kt):
        out = []
        for h in range(GLA_HEADS):
            pr = slice((h // 2) * LANES, (h // 2 + 1) * LANES)
            keep = head_lo if h % 2 == 0 else jnp.logical_not(head_lo)
            out.append(_dot_nt(jnp.where(keep, qt[:, pr], jnp.zeros((), BF16)), kt[:, pr]))
        return out

    blocks = [head_blocks(qt, kt) for qt, kt in operands]
    yield

    a = [2.0 * blocks[-1][h] for h in range(GLA_HEADS)]
    for w, blk in zip(levels, blocks[:-1]):
        sel = hb == w
        a = [jnp.where(sel, blk[h], a[h]) for h in range(GLA_HEADS)]

    outs = []
    for p in range(2):
        pr = slice(p * LANES, (p + 1) * LANES)
        vr = slice(p * 2 * GLA_DV, (p + 1) * 2 * GLA_DV)
        st = sb_ref[p]
        outs.append(_dot_nt(qe2[:, pr], st.astype(BF16)))
        upd = _dot_tn(v2[:, vr], ke2[:, pr])
        sb_ref[p] = dec2[:, pr] * st + jnp.where(sdiag, upd, 0.0)
    ob_ref[...] = jnp.concatenate(outs, axis=1)

    outs = []
    for p in range(2):
        pr = slice(p * LANES, (p + 1) * LANES)
        vr = slice(p * 2 * GLA_DV, (p + 1) * 2 * GLA_DV)
        st = sf_ref[p]
        inter = _dot_nt(qe[:, pr], st.astype(BF16))
        for hh in range(2):
            h = 2 * p + hh
            vh = slice(h * GLA_DV, (h + 1) * GLA_DV)
            outs.append(_dot(a[h].astype(BF16), v[:, vh]) + inter[:, hh * GLA_DV:(hh + 1) * GLA_DV])
        upd = _dot_tn(v[:, vr], ke[:, pr])
        sf_ref[p] = dec[:, pr] * st + jnp.where(sdiag, upd, 0.0)
    om_ref[...] = jnp.concatenate(outs, axis=1)
    yield


def _gla_call(gq, gk, gv, gf, gb, B, S):
    M = gq.shape[0]
    rows = GLA_SUB * GLA_C
    n = S // rows
    fw = lambda width: pl.BlockSpec((rows, width), lambda b, i: (b * n + i, 0))
    bw = lambda width: pl.BlockSpec((rows, width), lambda b, i: (b * n + n - 1 - i, 0))
    tables = _gla_tables(GLA_C)
    return pl.pallas_call(
        _gla_kernel, grid=(B, n),
        in_specs=[pl.BlockSpec(t.shape, lambda b, i: (0, 0)) for t in tables]
        + [fw(GLA_QK), fw(GLA_QK), fw(GLA_WIDTH), fw(GLA_QK), fw(GLA_QK),
           bw(GLA_QK), bw(GLA_QK), bw(GLA_WIDTH), bw(GLA_QK)],
        out_specs=[fw(GLA_WIDTH), bw(GLA_WIDTH)],
        out_shape=[jax.ShapeDtypeStruct((M, GLA_WIDTH), F32)] * 2,
        scratch_shapes=[pltpu.VMEM((2, 2 * GLA_DV, 2 * GLA_DK), F32)] * 2,
        compiler_params=pltpu.CompilerParams(dimension_semantics=("parallel", "arbitrary"),
                                             vmem_limit_bytes=VMEM_LIMIT),
        name="gla")(*tables, gq, gk, gv, gf, gb, gq, gk, gv, gb)


def _layer_norm(h, g, b):
    mu = jnp.mean(h, axis=-1, keepdims=True)
    d = h - mu
    var = jnp.mean(d * d, axis=-1, keepdims=True)
    return d * lax.rsqrt(var + LN_EPS) * g + b


def _merge_kernel(x_ref, ya_ref, om_ref, ob_ref, rs_ref, wg_ref, bg_ref, gn_ref,
                  wa_ref, wb_ref, wo_ref, lg_ref, lb_ref, o_ref, *, alpha):
    x = x_ref[...]
    gates = jax.nn.sigmoid(_dot(x.astype(BF16), wg_ref[...]) + bg_ref[...])
    o = om_ref[...] + ob_ref[...]
    gn = gn_ref[...]
    cols = []
    for h in range(GLA_HEADS):
        oh = o[:, h * GLA_DV:(h + 1) * GLA_DV]
        ms = jnp.mean(oh * oh, axis=-1, keepdims=True)
        cols.append(oh * lax.rsqrt(ms + RMS_EPS) * gn)
    yb = (jnp.concatenate(cols, axis=1) * rs_ref[...].astype(F32)).astype(BF16)
    pa = _dot(ya_ref[...], wa_ref[...])
    pb = _dot(yb, wb_ref[...])
    mixin = gates[:, :D_MODEL] * pa + gates[:, D_MODEL:] * pb
    mix = _dot(mixin.astype(BF16), wo_ref[...])
    o_ref[...] = _layer_norm(alpha * x + mix, lg_ref[...], lb_ref[...])


def _merge_call(x2, ya, om, ob, rs, w, alpha):
    M = x2.shape[0]
    tm = MERGE_TM
    full = lambda a: pl.BlockSpec(a.shape, lambda i: (0,) * a.ndim)
    tok = lambda n: pl.BlockSpec((tm, n), lambda i: (i, 0))
    ws = [w["wgate"], w["bgate"], w["gnorm"], w["wbra"], w["wbrb"], w["wout"], w["ln1g"], w["ln1b"]]
    return pl.pallas_call(
        functools.partial(_merge_kernel, alpha=alpha), grid=(M // tm,),
        in_specs=[tok(D_MODEL), tok(512), tok(512), tok(512), tok(512)] + [full(a) for a in ws],
        out_specs=tok(D_MODEL), out_shape=jax.ShapeDtypeStruct((M, D_MODEL), F32),
        compiler_params=pltpu.CompilerParams(dimension_semantics=("parallel",),
                                             vmem_limit_bytes=VMEM_LIMIT),
        name="merge")(x2, ya, om, ob, rs, *ws)


def _mlp_kernel(x_ref, w1_ref, w2_ref, lg_ref, lb_ref, o_ref, *, alpha):
    x = x_ref[...]
    h = jnp.maximum(_dot(x.astype(BF16), w1_ref[...]), 0.0)
    h = _dot((h * h).astype(BF16), w2_ref[...])
    o_ref[...] = _layer_norm(alpha * x + h, lg_ref[...], lb_ref[...])


def _mlp_call(x1, w, alpha):
    M = x1.shape[0]
    tm = MLP_TM
    once = lambda a: pl.BlockSpec(a.shape, lambda i: (0,) * a.ndim, pipeline_mode=pl.Buffered(1))
    tok = pl.BlockSpec((tm, D_MODEL), lambda i: (i, 0))
    ws = [w["w1"], w["w2"], w["ln2g"], w["ln2b"]]
    return pl.pallas_call(
        functools.partial(_mlp_kernel, alpha=alpha), grid=(M // tm,),
        in_specs=[tok] + [once(a) for a in ws],
        out_specs=tok, out_shape=jax.ShapeDtypeStruct((M, D_MODEL), F32),
        compiler_params=pltpu.CompilerParams(dimension_semantics=("parallel",),
                                             vmem_limit_bytes=VMEM_LIMIT),
        name="mlp")(x1, *ws)


def _rope_tables(S):
    half = ROT_DIM // 2
    inv = 1.0 / (ROPE_THETA ** (jnp.arange(0, ROT_DIM, 2, dtype=F32) / ROT_DIM))
    ang = jnp.arange(S, dtype=F32)[:, None] * inv[None, :]
    cos, sin = jnp.cos(ang), jnp.sin(ang)
    d = jnp.arange(LANES) % DA_DK
    idx = d % half
    cos_t = jnp.where(d[None, :] < ROT_DIM, cos[:, idx], 1.0)
    sin_a = jnp.where(d[None, :] < half, -sin[:, idx], 0.0)
    sin_b = jnp.where((d[None, :] >= half) & (d[None, :] < ROT_DIM), sin[:, idx], 0.0)
    return cos_t.astype(F32), sin_a.astype(F32), sin_b.astype(F32)


def _layer_weights(l, w_in, subln_g, w_dec_f, b_dec_f, w_dec_b, b_dec_b, gla_norm_g, w_br_a, w_br_b,
                   b_gate, w_out, ln1_g, ln1_b, w_mlp1, w_mlp2, ln2_g, ln2_b):
    offs = [0]
    for s in IN_SIZES:
        offs.append(offs[-1] + s)
    col = lambda i: w_in[l][:, offs[i]:offs[i + 1]]
    zeros = jnp.zeros((GLA_RANK, GLA_QK), F32)
    row = lambda a: a[l].reshape(1, -1).astype(F32)
    return {
        "wq": (col(0) * (DA_DK ** -0.5 * LOG2E)).astype(BF16), "wk": col(1).astype(BF16), "wv": col(2).astype(BF16),
        "wgq": (col(3) * GLA_DK ** -0.5).astype(BF16), "wgk": col(4).astype(BF16),
        "wgv": col(5).astype(BF16), "wgr": col(6).astype(BF16),
        "wz": jnp.concatenate([col(7), col(8)], axis=1).astype(BF16),
        "wdec": jnp.concatenate([jnp.concatenate([w_dec_f[l], zeros], axis=1),
                                 jnp.concatenate([zeros, w_dec_b[l]], axis=1)], axis=0).astype(BF16),
        "bdec": jnp.concatenate([b_dec_f[l], b_dec_b[l]]).reshape(1, -1).astype(F32),
        "wgate": col(9).astype(BF16), "bgate": row(b_gate),
        "subg": subln_g[l].reshape(-1, 1).astype(F32), "gnorm": row(gla_norm_g),
        "wbra": w_br_a[l].astype(BF16), "wbrb": w_br_b[l].astype(BF16), "wout": w_out[l].astype(BF16),
        "ln1g": row(ln1_g), "ln1b": row(ln1_b),
        "w1": w_mlp1[l].astype(BF16), "w2": w_mlp2[l].astype(BF16),
        "ln2g": row(ln2_g), "ln2b": row(ln2_b),
    }


def kernel(x_prompt, x_sample, w_in, lam_q1, lam_k1, lam_q2, lam_k2, subln_g, w_dec_f, b_dec_f, w_dec_b, b_dec_b, gla_norm_g, w_br_a, w_br_b, b_gate, w_out, ln1_g, ln1_b, w_mlp1, w_mlp2, ln2_g, ln2_b):
    depth = w_in.shape[0]
    alpha = (2 * depth) ** 0.25
    layers = []
    for l in range(depth):
        w = _layer_weights(l, w_in, subln_g, w_dec_f, b_dec_f, w_dec_b, b_dec_b, gla_norm_g,
                           w_br_a, w_br_b, b_gate, w_out, ln1_g, ln1_b, w_mlp1, w_mlp2, ln2_g, ln2_b)
        lamv = jnp.stack([lam_q1[l], lam_k1[l], lam_q2[l], lam_k2[l]]).astype(F32)
        layers.append((w, lamv, 0.8 - 0.6 * math.exp(-0.3 * l)))

    def trunk(x):
        B, S, D = x.shape
        tabs = _rope_tables(S)
        x2 = x.reshape(B * S, D)
        for w, lamv, lam_init in layers:
            q, k, vt, gq, gk, gv, rs, gf, gb = _proj_call(x2, B, S, w, tabs)
            ya = _attn_call(q.reshape(B, S, 512), k.reshape(B, S, 512), vt, lamv, w["subg"], lam_init)
            om, ob = _gla_call(gq, gk, gv, gf, gb, B, S)
            x1 = _merge_call(x2, ya.reshape(B * S, 512), om, ob, rs, w, alpha)
            x2 = _mlp_call(x1, w, alpha)
        return x2.reshape(B, S, D)

    return (trunk(x_prompt), trunk(x_sample))
```

```python
import functools
import math

import jax
import jax.numpy as jnp
import numpy as np
from jax import lax
from jax.experimental import pallas as pl
from jax.experimental.pallas import tpu as pltpu

F32 = jnp.float32
BF16 = jnp.bfloat16

D_MODEL = 1024
DA_HEADS = 4
DA_DK = 64
DA_DV = 128
DA_WIDTH = 512
ROT_DIM = 16
ROPE_THETA = 500000.0
GLA_HEADS = 4
GLA_DK = 64
GLA_DV = 128
GLA_QK = 256
GLA_WIDTH = 512
GLA_RANK = 16
GLA_TAU = 16.0
D_FF = 4096
LN_EPS = 1e-5
RMS_EPS = 1e-6
IN_SIZES = (512, 512, 512, 256, 256, 512, 512, 16, 16, 2048)

LANES = 128
VMEM_LIMIT = 56 * 1024 * 1024

PROJ_TM = 512
ATT_TQ = 128
ATT_TK = 1024
ATT_RC = 128
MXU_DIM = 256
LOG2E = math.log2(math.e)
GLA_C = 128
GLA_SUB = 4
MERGE_TM = 512
MLP_TM = 512
ROW_GROUPS = 2


def _dot(a, b):
    return jnp.dot(a, b, preferred_element_type=F32)


def _dot_nt(a, b):
    return lax.dot_general(a, b, (((1,), (1,)), ((), ())), preferred_element_type=F32)


def _dot_tn(a, b):
    return lax.dot_general(a, b, (((0,), (0,)), ((), ())), preferred_element_type=F32)


def _proj_kernel(x_ref, wq_ref, wk_ref, wv_ref, wgq_ref, wgk_ref, wgv_ref, wgr_ref, wz_ref,
                 wdec_ref, bdec_ref, cos_ref, sa_ref, sb_ref,
                 q_ref, k_ref, vt_ref, gq_ref, gk_ref, gv_ref, rs_ref, gf_ref, gb_ref):
    x = x_ref[...].astype(BF16)
    cos = cos_ref[...]
    sin_a = sa_ref[...]
    sin_b = sb_ref[...]

    def rope(t):
        cols = []
        for j in range(t.shape[1] // LANES):
            blk = t[:, j * LANES:(j + 1) * LANES]
            cols.append(blk * cos + pltpu.roll(blk, LANES - ROT_DIM // 2, 1) * sin_a
                        + pltpu.roll(blk, ROT_DIM // 2, 1) * sin_b)
        return jnp.concatenate(cols, axis=1)

    z = _dot(x, wz_ref[...])
    zz = _dot(z.astype(BF16), wdec_ref[...]) + bdec_ref[...]
    g = -(jnp.maximum(-zz, 0.0) + jnp.log1p(jnp.exp(-jnp.abs(zz)))) / GLA_TAU
    gf_ref[...] = g[:, :GLA_QK]
    gb_ref[...] = g[:, GLA_QK:]
    r = _dot(x, wgr_ref[...])
    rs_ref[...] = (r * jax.nn.sigmoid(r)).astype(BF16)
    q_ref[...] = rope(_dot(x, wq_ref[...])).astype(BF16)
    k_ref[...] = rope(_dot(x, wk_ref[...])).astype(BF16)
    vt_ref[...] = _dot(x, wv_ref[...]).T.astype(BF16)
    gq_ref[...] = _dot(x, wgq_ref[...]).astype(BF16)
    gk_ref[...] = _dot(x, wgk_ref[...]).astype(BF16)
    gv_ref[...] = _dot(x, wgv_ref[...]).astype(BF16)


def _proj_call(x2, B, S, w, tabs):
    M = x2.shape[0]
    tm = PROJ_TM
    ns = S // tm
    full = lambda a: pl.BlockSpec(a.shape, lambda i: (0,) * a.ndim)
    tok = lambda n: pl.BlockSpec((tm, n), lambda i: (i, 0))
    tab = pl.BlockSpec((tm, LANES), lambda i: (i % ns, 0))
    ins = [x2, w["wq"], w["wk"], w["wv"], w["wgq"], w["wgk"], w["wgv"], w["wgr"], w["wz"],
           w["wdec"], w["bdec"], tabs[0], tabs[1], tabs[2]]
    in_specs = [tok(D_MODEL)] + [full(a) for a in ins[1:11]] + [tab, tab, tab]
    out_shape = [
        jax.ShapeDtypeStruct((M, 512), BF16),
        jax.ShapeDtypeStruct((M, 512), BF16),
        jax.ShapeDtypeStruct((B, 512, S), BF16),
        jax.ShapeDtypeStruct((M, GLA_QK), BF16),
        jax.ShapeDtypeStruct((M, GLA_QK), BF16),
        jax.ShapeDtypeStruct((M, GLA_WIDTH), BF16),
        jax.ShapeDtypeStruct((M, GLA_WIDTH), BF16),
        jax.ShapeDtypeStruct((M, GLA_QK), F32),
        jax.ShapeDtypeStruct((M, GLA_QK), F32),
    ]
    out_specs = [tok(512), tok(512),
                 pl.BlockSpec((None, 512, tm), lambda i: (i // ns, 0, i % ns)),
                 tok(GLA_QK), tok(GLA_QK), tok(GLA_WIDTH), tok(GLA_WIDTH), tok(GLA_QK), tok(GLA_QK)]
    return pl.pallas_call(
        _proj_kernel, grid=(M // tm,), in_specs=in_specs, out_specs=out_specs, out_shape=out_shape,
        compiler_params=pltpu.CompilerParams(dimension_semantics=("parallel",),
                                             vmem_limit_bytes=VMEM_LIMIT),
        name="proj")(*ins)


def _attn_kernel(lamv_ref, g_ref, q_ref, k_ref, vt_ref, o_ref, s0_ref, s1_ref, p0_ref, p1_ref,
                 *, seq, lam_init):
    tq, tk, rc = ATT_TQ, ATT_TK, ATT_RC
    ncol = 2 * tq
    nch = tk // rc
    n = seq // tk
    nq = seq // tq
    nqk = tk // MXU_DIM
    s_bufs = (s0_ref, s1_ref)
    p_bufs = (p0_ref, p1_ref)
    acc_rows = MXU_DIM // 4
    pv_base = 2 * acc_rows

    lv = lamv_ref[...]
    lam = (jnp.exp(jnp.sum(lv[0:1] * lv[1:2], keepdims=True))
           - jnp.exp(jnp.sum(lv[2:3] * lv[3:4], keepdims=True)) + lam_init)
    gain = g_ref[...] * (1.0 - lam_init)
    k_pad = jnp.zeros((MXU_DIM, MXU_DIM - LANES), BF16)

    def make_qq(qi):
        qt = q_ref[pl.ds(pl.multiple_of(qi * tq, tq), tq), :].astype(F32).T
        row = lax.broadcasted_iota(jnp.int32, qt.shape, 0)
        qq = jnp.concatenate([jnp.where(row < DA_DK, qt, 0.0),
                              jnp.where(row >= DA_DK, qt, 0.0)], axis=1).astype(BF16)
        return jnp.concatenate([qq, jnp.zeros((MXU_DIM - LANES, ncol), BF16)], axis=0)

    def mxu_stage(qq, t, pv_t, alpha, acc):
        def key_rows(j):
            lo = t * tk + j * MXU_DIM
            return jnp.concatenate([k_ref[lo:lo + MXU_DIM, :], k_pad], axis=1)

        def score_addr(j):
            return acc_rows * (((nqk - 1) * t + j) % 4) if j < nqk - 1 else acc_rows * (t % 2)

        if t is not None:
            if t == 0:
                pltpu.matmul_push_rhs(qq, staging_register=0, mxu_index=0)
            for j in range(nqk - 1):
                pltpu.matmul_acc_lhs(score_addr(j), key_rows(j), mxu_index=0,
                                     load_staged_rhs=0 if (t == 0 and j == 0) else None)
        if pv_t is not None:
            pv_addr = pv_base + (DA_DV // 4) * (pv_t % 2)
            for kt in range(tk // MXU_DIM):
                lo = kt * MXU_DIM
                pltpu.matmul_push_rhs(p_bufs[pv_t % 2][lo:lo + MXU_DIM, :], staging_register=kt % 2,
                                      mxu_index=1)
                pltpu.matmul_acc_lhs(pv_addr, vt_ref[:, pv_t * tk + lo:pv_t * tk + lo + MXU_DIM],
                                     mxu_index=1, load_staged_rhs=kt % 2)
        if t is not None:
            pltpu.matmul_push_rhs(qq, staging_register=0, mxu_index=1)
            pltpu.matmul_acc_lhs(score_addr(nqk - 1), key_rows(nqk - 1), mxu_index=1, load_staged_rhs=0)

        mx = None
        if t is not None:
            for j in range(nqk - 1):
                sc = pltpu.matmul_pop(score_addr(j), (MXU_DIM, ncol), F32, 0)
                s_bufs[t % 2][j * MXU_DIM:(j + 1) * MXU_DIM, :] = sc
                part = jnp.max(sc.reshape(MXU_DIM // 8, 8, ncol), axis=0)
                mx = part if mx is None else jnp.maximum(mx, part)
        if pv_t is not None:
            acc = alpha * acc + pltpu.matmul_pop(pv_addr, (DA_DV, ncol), F32, 1)
        if t is not None:
            sc = pltpu.matmul_pop(score_addr(nqk - 1), (MXU_DIM, ncol), F32, 1)
            s_bufs[t % 2][(nqk - 1) * MXU_DIM:, :] = sc
            mx = jnp.maximum(mx, jnp.max(sc.reshape(MXU_DIM // 8, 8, ncol), axis=0))
        return mx, acc

    def softmax_stage(t, mx, m, l):
        m_new = jnp.maximum(m, jnp.max(mx, axis=0, keepdims=True))
        alpha = jnp.exp2(m - m_new)
        mb = jnp.broadcast_to(m_new, (8, ncol))[None]
        ls = None
        for c in range(nch):
            rows = slice(c * rc, (c + 1) * rc)
            p = jnp.exp2(s_bufs[t % 2][rows, :].reshape(rc // 8, 8, ncol) - mb)
            part = jnp.sum(p, axis=0)
            ls = part if ls is None else ls + part
            p_bufs[t % 2][rows, :] = p.reshape(rc, ncol).astype(BF16)
        return m_new, alpha * l + jnp.sum(ls, axis=0, keepdims=True), alpha

    def finalize(qi, l, acc):
        on = acc / l
        o = on[:, :tq] - lam * on[:, tq:]
        ms = jnp.mean(o * o, axis=0, keepdims=True)
        y = o * lax.rsqrt(ms + RMS_EPS) * gain
        o_ref[pl.ds(pl.multiple_of(qi * tq, tq), tq), :] = y.T.astype(BF16)

    m_init = jnp.full((1, ncol), -jnp.inf, F32)
    l_init = jnp.zeros((1, ncol), F32)
    acc_init = jnp.zeros((DA_DV, ncol), F32)

    def query_tile(qi, qq, state, first):
        mx, m, l, alpha, acc = state
        for t in range(n):
            if t == 0 and not first:
                mx_new, acc = mxu_stage(qq, t, n - 2, alpha, acc)
                m, l, alpha = softmax_stage(n - 1, mx, m, l)
            elif t == 1:
                mx_new, acc = mxu_stage(qq, t, None if first else n - 1, alpha, acc)
                if not first:
                    finalize(qi - 1, l, acc)
                m, l, alpha = softmax_stage(0, mx, m_init, l_init)
                acc = acc_init
            elif t >= 2:
                mx_new, acc = mxu_stage(qq, t, t - 2, alpha, acc)
                m, l, alpha = softmax_stage(t - 1, mx, m, l)
            else:
                mx_new, acc = mxu_stage(qq, t, None, alpha, acc)
            mx = mx_new
        return mx, m, l, alpha, acc

    state = (jnp.zeros((8, ncol), F32), m_init, l_init, jnp.ones((1, ncol), F32), acc_init)
    state = query_tile(0, make_qq(0), state, True)

    def body(qi, carry):
        qq, state = carry
        qq_next = make_qq(jnp.minimum(qi + 1, nq - 1))
        return qq_next, query_tile(qi, qq, state, False)

    _, state = lax.fori_loop(1, nq, body, (make_qq(1), state))
    mx, m, l, alpha, acc = state
    _, acc = mxu_stage(None, None, n - 2, alpha, acc)
    m, l, alpha = softmax_stage(n - 1, mx, m, l)
    _, acc = mxu_stage(None, None, n - 1, alpha, acc)
    finalize(nq - 1, l, acc)


def _attn_call(q, k, vt, lamv, subg, lam_init):
    B, S, _ = q.shape
    seq_blk = pl.BlockSpec((None, S, LANES), lambda b, h: (b, 0, h))
    return pl.pallas_call(
        functools.partial(_attn_kernel, seq=S, lam_init=lam_init),
        grid=(B, DA_HEADS),
        in_specs=[
            pl.BlockSpec(lamv.shape, lambda b, h: (0, 0)),
            pl.BlockSpec(subg.shape, lambda b, h: (0, 0)),
            seq_blk, seq_blk,
            pl.BlockSpec((None, LANES, S), lambda b, h: (b, h, 0)),
        ],
        out_specs=seq_blk,
        out_shape=jax.ShapeDtypeStruct((B, S, DA_WIDTH), BF16),
        scratch_shapes=[pltpu.VMEM((ATT_TK, 2 * ATT_TQ), F32)] * 2
        + [pltpu.VMEM((ATT_TK, 2 * ATT_TQ), BF16)] * 2,
        compiler_params=pltpu.CompilerParams(
            dimension_semantics=("parallel", "parallel"),
            vmem_limit_bytes=VMEM_LIMIT),
        name="diff_attn")(lamv, subg, q, k, vt)


GLA_MXU_LEVELS = (4, 2, 1)


def _split2(g):
    g1 = g.astype(BF16)
    return g1, (g - g1.astype(F32)).astype(BF16)


def _gla_tables(C):
    t = np.arange(C)
    tri_i = (t[None, :] <= t[:, None]).astype(np.float32)
    tri_s = (t[None, :] < t[:, None]).astype(np.float32)
    blocks = []
    for w in GLA_MXU_LEVELS:
        rho = (t // (2 * w)) * 2 * w + w
        odd = ((t & w) != 0).astype(np.float32)[:, None]
        d_f = tri_i - tri_i[rho]
        d_b = tri_s - tri_s[rho]
        blocks.append(np.concatenate([odd * d_f, (odd - 1.0) * d_b], 1))
        blocks.append(np.concatenate([(odd - 1.0) * d_f, odd * d_b], 1))
    tab = np.concatenate(blocks, 0)
    x = t[:, None] ^ t[None, :]
    hb = np.where(x > 0, 2 ** np.floor(np.log2(np.maximum(x, 1))), 0).astype(np.int32)
    tri = np.concatenate([tri_i, tri_s], 0)
    return (jnp.asarray(np.concatenate([tab, tab], 1), BF16),
            jnp.asarray(np.concatenate([tri, tri], 1), BF16), jnp.asarray(hb))


def _pair_ref_rows(c, w):
    C, n = c.shape
    c3 = c.reshape(C // (2 * w), 2 * w, n)
    return jnp.broadcast_to(c3[:, w:w + 1, :], c3.shape).reshape(C, n)


def _gla_kernel(tab_ref, tri_ref, hb_ref, q_ref, k_ref, v_ref, gf_ref, gb_ref,
                qb_ref, kb_ref, vb_ref, gbb_ref, om_ref, ob_ref, sf_ref, sb_ref):
    @pl.when(pl.program_id(1) == 0)
    def _():
        sf_ref[...] = jnp.zeros_like(sf_ref)
        sb_ref[...] = jnp.zeros_like(sb_ref)

    chunks = []
    for j in range(GLA_SUB):
        rf = pl.ds(j * GLA_C, GLA_C)
        rb = pl.ds((GLA_SUB - 1 - j) * GLA_C, GLA_C)
        chunks.append(_gla_chunk(tab_ref, tri_ref, hb_ref, q_ref.at[rf], k_ref.at[rf], v_ref.at[rf],
                                 gf_ref.at[rf], gb_ref.at[rf], qb_ref.at[rb], kb_ref.at[rb],
                                 vb_ref.at[rb], gbb_ref.at[rb], om_ref.at[rf], ob_ref.at[rb],
                                 sf_ref, sb_ref))
    for _ in range(GLA_PHASES):
        for chunk in chunks:
            next(chunk)


GLA_PHASES = 3


def _gla_chunk(tab_ref, tri_ref, hb_ref, q_ref, k_ref, v_ref, gf_ref, gb_ref,
               qb_ref, kb_ref, vb_ref, gbb_ref, om_ref, ob_ref, sf_ref, sb_ref):
    C = GLA_C
    hb = hb_ref[...]
    rowi = lax.broadcasted_iota(jnp.int32, (C, 1), 0)
    lane = lax.broadcasted_iota(jnp.int32, (C, LANES), 1)
    head_lo = lane < GLA_DK
    sr = lax.broadcasted_iota(jnp.int32, (2 * GLA_DV, 2 * GLA_DK), 0)
    sc = lax.broadcasted_iota(jnp.int32, (2 * GLA_DV, 2 * GLA_DK), 1)
    sdiag = (sr < GLA_DV) == (sc < GLA_DK)

    q2 = qb_ref[...].astype(F32)
    k2 = kb_ref[...].astype(F32)
    v2 = vb_ref[...]
    g2 = gbb_ref[...]
    cx = _dot(tri_ref[C:2 * C, :], jnp.concatenate(_split2(g2), axis=0))
    tot = cx[C - 1:C, :] + g2[C - 1:C, :]
    qe2 = (q2 * jnp.exp(tot - cx)).astype(BF16)
    ke2 = (k2 * jnp.exp(cx)).astype(BF16)
    dec2 = jnp.exp(tot)

    q = q_ref[...].astype(F32)
    k = k_ref[...].astype(F32)
    v = v_ref[...]
    gf1, gf2 = _split2(gf_ref[...])
    gb1, gb2 = _split2(gb_ref[...])
    cf = _dot(tri_ref[0:C, :], jnp.concatenate([gf1, gf2], axis=0))
    cb = _dot(tri_ref[C:2 * C, :], jnp.concatenate([gb1, gb2], axis=0))
    g4 = jnp.concatenate([gf1, gb1, gf2, gb2], axis=0)

    levels = [C >> (i + 1) for i in range(C.bit_length() - 1)]
    operands = []
    for w in levels:
        if w in GLA_MXU_LEVELS:
            base = 2 * C * GLA_MXU_LEVELS.index(w)
            arg = _dot(tab_ref[base:base + 2 * C, :], g4)
            arg_q, arg_k = arg[:C], arg[C:]
        else:
            odd = (rowi & w) != 0
            d_f = cf - _pair_ref_rows(cf, w)
            d_b = cb - _pair_ref_rows(cb, w)
            arg_q = jnp.where(odd, d_f, -d_b)
            arg_k = jnp.where(odd, d_b, -d_f)
        operands.append(((q * jnp.exp(arg_q)).astype(BF16), (k * jnp.exp(arg_k)).astype(BF16)))
    operands.append((q_ref[...], k_ref[...]))
    cf_last = cf[C - 1:C, :]
    qe = (q * jnp.exp(cf)).astype(BF16)
    ke = (k * jnp.exp(cf_last - cf)).astype(BF16)
    dec = jnp.exp(cf_last)
    yield

    def head_blocks(qt, kt):
        out = []
        for h in range(GLA_HEADS):
            pr = slice((h // 2) * LANES, (h // 2 + 1) * LANES)
            keep = head_lo if h % 2 == 0 else jnp.logical_not(head_lo)
            out.append(_dot_nt(jnp.where(keep, qt[:, pr], jnp.zeros((), BF16)), kt[:, pr]))
        return out

    blocks = [head_blocks(qt, kt) for qt, kt in operands]
    yield

    a = [2.0 * blocks[-1][h] for h in range(GLA_HEADS)]
    for w, blk in zip(levels, blocks[:-1]):
        sel = hb == w
        a = [jnp.where(sel, blk[h], a[h]) for h in range(GLA_HEADS)]

    outs = []
    for p in range(2):
        pr = slice(p * LANES, (p + 1) * LANES)
        vr = slice(p * 2 * GLA_DV, (p + 1) * 2 * GLA_DV)
        st = sb_ref[p]
        outs.append(_dot_nt(qe2[:, pr], st.astype(BF16)))
        upd = _dot_tn(v2[:, vr], ke2[:, pr])
        sb_ref[p] = dec2[:, pr] * st + jnp.where(sdiag, upd, 0.0)
    ob_ref[...] = jnp.concatenate(outs, axis=1)

    outs = []
    for p in range(2):
        pr = slice(p * LANES, (p + 1) * LANES)
        vr = slice(p * 2 * GLA_DV, (p + 1) * 2 * GLA_DV)
        st = sf_ref[p]
        inter = _dot_nt(qe[:, pr], st.astype(BF16))
        for hh in range(2):
            h = 2 * p + hh
            vh = slice(h * GLA_DV, (h + 1) * GLA_DV)
            outs.append(_dot(a[h].astype(BF16), v[:, vh]) + inter[:, hh * GLA_DV:(hh + 1) * GLA_DV])
        upd = _dot_tn(v[:, vr], ke[:, pr])
        sf_ref[p] = dec[:, pr] * st + jnp.where(sdiag, upd, 0.0)
    om_ref[...] = jnp.concatenate(outs, axis=1)
    yield


def _gla_call(gq, gk, gv, gf, gb, B, S):
    M = gq.shape[0]
    rows = GLA_SUB * GLA_C
    n = S // rows
    fw = lambda width: pl.BlockSpec((rows, width), lambda b, i: (b * n + i, 0))
    bw = lambda width: pl.BlockSpec((rows, width), lambda b, i: (b * n + n - 1 - i, 0))
    tables = _gla_tables(GLA_C)
    return pl.pallas_call(
        _gla_kernel, grid=(B, n),
        in_specs=[pl.BlockSpec(t.shape, lambda b, i: (0, 0)) for t in tables]
        + [fw(GLA_QK), fw(GLA_QK), fw(GLA_WIDTH), fw(GLA_QK), fw(GLA_QK),
           bw(GLA_QK), bw(GLA_QK), bw(GLA_WIDTH), bw(GLA_QK)],
        out_specs=[fw(GLA_WIDTH), bw(GLA_WIDTH)],
        out_shape=[jax.ShapeDtypeStruct((M, GLA_WIDTH), F32)] * 2,
        scratch_shapes=[pltpu.VMEM((2, 2 * GLA_DV, 2 * GLA_DK), F32)] * 2,
        compiler_params=pltpu.CompilerParams(dimension_semantics=("parallel", "arbitrary"),
                                             vmem_limit_bytes=VMEM_LIMIT),
        name="gla")(*tables, gq, gk, gv, gf, gb, gq, gk, gv, gb)


def _layer_norm(h, g, b):
    mu = jnp.mean(h, axis=-1, keepdims=True)
    d = h - mu
    var = jnp.mean(d * d, axis=-1, keepdims=True)
    return d * lax.rsqrt(var + LN_EPS) * g + b


def _lockstep(parts):
    parts = list(parts)
    while parts:
        parts = [p for p in parts if next(p, StopIteration) is not StopIteration]


def _row_groups(ref, count):
    rows = ref.shape[0] // count
    return [ref.at[pl.ds(i * rows, rows)] for i in range(count)]


def _merge_rows(x_ref, ya_ref, om_ref, ob_ref, rs_ref, wg_ref, bg_ref, gn_ref,
                wa_ref, wb_ref, wo_ref, lg_ref, lb_ref, o_ref, alpha):
    x = x_ref[...]
    gate_logits = _dot(x.astype(BF16), wg_ref[...]) + bg_ref[...]
    pa = _dot(ya_ref[...], wa_ref[...])
    o = om_ref[...] + ob_ref[...]
    gn = gn_ref[...]
    cols = []
    for h in range(GLA_HEADS):
        oh = o[:, h * GLA_DV:(h + 1) * GLA_DV]
        ms = jnp.mean(oh * oh, axis=-1, keepdims=True)
        cols.append(oh * lax.rsqrt(ms + RMS_EPS) * gn)
    yb = (jnp.concatenate(cols, axis=1) * rs_ref[...].astype(F32)).astype(BF16)
    yield
    pb = _dot(yb, wb_ref[...])
    gates = jax.nn.sigmoid(gate_logits)
    mixin = (gates[:, :D_MODEL] * pa + gates[:, D_MODEL:] * pb).astype(BF16)
    yield
    mix = _dot(mixin, wo_ref[...])
    yield
    o_ref[...] = _layer_norm(alpha * x + mix, lg_ref[...], lb_ref[...])


def _merge_kernel(x_ref, ya_ref, om_ref, ob_ref, rs_ref, wg_ref, bg_ref, gn_ref,
                  wa_ref, wb_ref, wo_ref, lg_ref, lb_ref, o_ref, *, alpha):
    tiles = [_row_groups(r, ROW_GROUPS) for r in (x_ref, ya_ref, om_ref, ob_ref, rs_ref, o_ref)]
    _lockstep(_merge_rows(x, ya, om, ob, rs, wg_ref, bg_ref, gn_ref, wa_ref, wb_ref, wo_ref,
                          lg_ref, lb_ref, o, alpha) for x, ya, om, ob, rs, o in zip(*tiles))


def _merge_call(x2, ya, om, ob, rs, w, alpha):
    M = x2.shape[0]
    tm = MERGE_TM
    full = lambda a: pl.BlockSpec(a.shape, lambda i: (0,) * a.ndim)
    tok = lambda n: pl.BlockSpec((tm, n), lambda i: (i, 0))
    ws = [w["wgate"], w["bgate"], w["gnorm"], w["wbra"], w["wbrb"], w["wout"], w["ln1g"], w["ln1b"]]
    return pl.pallas_call(
        functools.partial(_merge_kernel, alpha=alpha), grid=(M // tm,),
        in_specs=[tok(D_MODEL), tok(512), tok(512), tok(512), tok(512)] + [full(a) for a in ws],
        out_specs=tok(D_MODEL), out_shape=jax.ShapeDtypeStruct((M, D_MODEL), F32),
        compiler_params=pltpu.CompilerParams(dimension_semantics=("parallel",),
                                             vmem_limit_bytes=VMEM_LIMIT),
        name="merge")(x2, ya, om, ob, rs, *ws)


def _mlp_kernel(x_ref, w1_ref, w2_ref, lg_ref, lb_ref, o_ref, *, alpha):
    x = x_ref[...]
    h = jnp.maximum(_dot(x.astype(BF16), w1_ref[...]), 0.0)
    h = _dot((h * h).astype(BF16), w2_ref[...])
    o_ref[...] = _layer_norm(alpha * x + h, lg_ref[...], lb_ref[...])


def _mlp_call(x1, w, alpha):
    M = x1.shape[0]
    tm = MLP_TM
    once = lambda a: pl.BlockSpec(a.shape, lambda i: (0,) * a.ndim, pipeline_mode=pl.Buffered(1))
    tok = pl.BlockSpec((tm, D_MODEL), lambda i: (i, 0))
    ws = [w["w1"], w["w2"], w["ln2g"], w["ln2b"]]
    return pl.pallas_call(
        functools.partial(_mlp_kernel, alpha=alpha), grid=(M // tm,),
        in_specs=[tok] + [once(a) for a in ws],
        out_specs=tok, out_shape=jax.ShapeDtypeStruct((M, D_MODEL), F32),
        compiler_params=pltpu.CompilerParams(dimension_semantics=("parallel",),
                                             vmem_limit_bytes=VMEM_LIMIT),
        name="mlp")(x1, *ws)


def _rope_tables(S):
    half = ROT_DIM // 2
    inv = 1.0 / (ROPE_THETA ** (jnp.arange(0, ROT_DIM, 2, dtype=F32) / ROT_DIM))
    ang = jnp.arange(S, dtype=F32)[:, None] * inv[None, :]
    cos, sin = jnp.cos(ang), jnp.sin(ang)
    d = jnp.arange(LANES) % DA_DK
    idx = d % half
    cos_t = jnp.where(d[None, :] < ROT_DIM, cos[:, idx], 1.0)
    sin_a = jnp.where(d[None, :] < half, -sin[:, idx], 0.0)
    sin_b = jnp.where((d[None, :] >= half) & (d[None, :] < ROT_DIM), sin[:, idx], 0.0)
    return cos_t.astype(F32), sin_a.astype(F32), sin_b.astype(F32)


def _layer_weights(l, w_in, subln_g, w_dec_f, b_dec_f, w_dec_b, b_dec_b, gla_norm_g, w_br_a, w_br_b,
                   b_gate, w_out, ln1_g, ln1_b, w_mlp1, w_mlp2, ln2_g, ln2_b):
    offs = [0]
    for s in IN_SIZES:
        offs.append(offs[-1] + s)
    col = lambda i: w_in[l][:, offs[i]:offs[i + 1]]
    zeros = jnp.zeros((GLA_RANK, GLA_QK), F32)
    row = lambda a: a[l].reshape(1, -1).astype(F32)
    return {
        "wq": (col(0) * (DA_DK ** -0.5 * LOG2E)).astype(BF16), "wk": col(1).astype(BF16), "wv": col(2).astype(BF16),
        "wgq": (col(3) * GLA_DK ** -0.5).astype(BF16), "wgk": col(4).astype(BF16),
        "wgv": col(5).astype(BF16), "wgr": col(6).astype(BF16),
        "wz": jnp.concatenate([col(7), col(8)], axis=1).astype(BF16),
        "wdec": jnp.concatenate([jnp.concatenate([w_dec_f[l], zeros], axis=1),
                                 jnp.concatenate([zeros, w_dec_b[l]], axis=1)], axis=0).astype(BF16),
        "bdec": jnp.concatenate([b_dec_f[l], b_dec_b[l]]).reshape(1, -1).astype(F32),
        "wgate": col(9).astype(BF16), "bgate": row(b_gate),
        "subg": subln_g[l].reshape(-1, 1).astype(F32), "gnorm": row(gla_norm_g),
        "wbra": w_br_a[l].astype(BF16), "wbrb": w_br_b[l].astype(BF16), "wout": w_out[l].astype(BF16),
        "ln1g": row(ln1_g), "ln1b": row(ln1_b),
        "w1": w_mlp1[l].astype(BF16), "w2": w_mlp2[l].astype(BF16),
        "ln2g": row(ln2_g), "ln2b": row(ln2_b),
    }


def kernel(x_prompt, x_sample, w_in, lam_q1, lam_k1, lam_q2, lam_k2, subln_g, w_dec_f, b_dec_f, w_dec_b, b_dec_b, gla_norm_g, w_br_a, w_br_b, b_gate, w_out, ln1_g, ln1_b, w_mlp1, w_mlp2, ln2_g, ln2_b):
    depth = w_in.shape[0]
    alpha = (2 * depth) ** 0.25
    layers = []
    for l in range(depth):
        w = _layer_weights(l, w_in, subln_g, w_dec_f, b_dec_f, w_dec_b, b_dec_b, gla_norm_g,
                           w_br_a, w_br_b, b_gate, w_out, ln1_g, ln1_b, w_mlp1, w_mlp2, ln2_g, ln2_b)
        lamv = jnp.stack([lam_q1[l], lam_k1[l], lam_q2[l], lam_k2[l]]).astype(F32)
        layers.append((w, lamv, 0.8 - 0.6 * math.exp(-0.3 * l)))

    def trunk(x):
        B, S, D = x.shape
        tabs = _rope_tables(S)
        x2 = x.reshape(B * S, D)
        for w, lamv, lam_init in layers:
            q, k, vt, gq, gk, gv, rs, gf, gb = _proj_call(x2, B, S, w, tabs)
            ya = _attn_call(q.reshape(B, S, 512), k.reshape(B, S, 512), vt, lamv, w["subg"], lam_init)
            om, ob = _gla_call(gq, gk, gv, gf, gb, B, S)
            x1 = _merge_call(x2, ya.reshape(B * S, 512), om, ob, rs, w, alpha)
            x2 = _mlp_call(x1, w, alpha)
        return x2.reshape(B, S, D)

    return (trunk(x_prompt), trunk(x_sample))
```

```python
import functools
import math

import jax
import jax.numpy as jnp
import numpy as np
from jax import lax
from jax.experimental import pallas as pl
from jax.experimental.pallas import tpu as pltpu

F32 = jnp.float32
BF16 = jnp.bfloat16

D_MODEL = 1024
DA_HEADS = 4
DA_DK = 64
DA_DV = 128
DA_WIDTH = 512
ROT_DIM = 16
ROPE_THETA = 500000.0
GLA_HEADS = 4
GLA_DK = 64
GLA_DV = 128
GLA_QK = 256
GLA_WIDTH = 512
GLA_RANK = 16
GLA_TAU = 16.0
D_FF = 4096
LN_EPS = 1e-5
RMS_EPS = 1e-6
IN_SIZES = (512, 512, 512, 256, 256, 512, 512, 16, 16, 2048)

LANES = 128
VMEM_LIMIT = 56 * 1024 * 1024

PROJ_TM = 512
ATT_TQ = 128
ATT_TK = 1024
ATT_RC = 128
LOG2E = math.log2(math.e)
GLA_C = 128
GLA_SUB = 4
MERGE_TM = 512
MLP_TM = 512
ROW_GROUPS = 2


def _dot(a, b):
    return jnp.dot(a, b, preferred_element_type=F32)


def _dot_nt(a, b):
    return lax.dot_general(a, b, (((1,), (1,)), ((), ())), preferred_element_type=F32)


def _dot_tn(a, b):
    return lax.dot_general(a, b, (((0,), (0,)), ((), ())), preferred_element_type=F32)


def _proj_kernel(x_ref, wq_ref, wk_ref, wv_ref, wgq_ref, wgk_ref, wgv_ref, wgr_ref, wz_ref,
                 wdec_ref, bdec_ref, cos_ref, sa_ref, sb_ref,
                 q_ref, k_ref, vt_ref, gq_ref, gk_ref, gv_ref, rs_ref, gf_ref, gb_ref):
    x = x_ref[...].astype(BF16)
    cos = cos_ref[...]
    sin_a = sa_ref[...]
    sin_b = sb_ref[...]

    def rope(t):
        cols = []
        for j in range(t.shape[1] // LANES):
            blk = t[:, j * LANES:(j + 1) * LANES]
            cols.append(blk * cos + pltpu.roll(blk, LANES - ROT_DIM // 2, 1) * sin_a
                        + pltpu.roll(blk, ROT_DIM // 2, 1) * sin_b)
        return jnp.concatenate(cols, axis=1)

    z = _dot(x, wz_ref[...])
    zz = _dot(z.astype(BF16), wdec_ref[...]) + bdec_ref[...]
    g = -(jnp.maximum(-zz, 0.0) + jnp.log1p(jnp.exp(-jnp.abs(zz)))) / GLA_TAU
    gf_ref[...] = g[:, :GLA_QK]
    gb_ref[...] = g[:, GLA_QK:]
    r = _dot(x, wgr_ref[...])
    rs_ref[...] = (r * jax.nn.sigmoid(r)).astype(BF16)
    q_ref[...] = rope(_dot(x, wq_ref[...])).astype(BF16)
    k_ref[...] = rope(_dot(x, wk_ref[...])).astype(BF16)
    vt_ref[...] = _dot(x, wv_ref[...]).T.astype(BF16)
    gq_ref[...] = _dot(x, wgq_ref[...]).astype(BF16)
    gk_ref[...] = _dot(x, wgk_ref[...]).astype(BF16)
    gv_ref[...] = _dot(x, wgv_ref[...]).astype(BF16)


def _proj_call(x2, B, S, w, tabs):
    M = x2.shape[0]
    tm = PROJ_TM
    ns = S // tm
    full = lambda a: pl.BlockSpec(a.shape, lambda i: (0,) * a.ndim)
    tok = lambda n: pl.BlockSpec((tm, n), lambda i: (i, 0))
    tab = pl.BlockSpec((tm, LANES), lambda i: (i % ns, 0))
    ins = [x2, w["wq"], w["wk"], w["wv"], w["wgq"], w["wgk"], w["wgv"], w["wgr"], w["wz"],
           w["wdec"], w["bdec"], tabs[0], tabs[1], tabs[2]]
    in_specs = [tok(D_MODEL)] + [full(a) for a in ins[1:11]] + [tab, tab, tab]
    out_shape = [
        jax.ShapeDtypeStruct((M, 512), BF16),
        jax.ShapeDtypeStruct((M, 512), BF16),
        jax.ShapeDtypeStruct((B, 512, S), BF16),
        jax.ShapeDtypeStruct((M, GLA_QK), BF16),
        jax.ShapeDtypeStruct((M, GLA_QK), BF16),
        jax.ShapeDtypeStruct((M, GLA_WIDTH), BF16),
        jax.ShapeDtypeStruct((M, GLA_WIDTH), BF16),
        jax.ShapeDtypeStruct((M, GLA_QK), F32),
        jax.ShapeDtypeStruct((M, GLA_QK), F32),
    ]
    out_specs = [tok(512), tok(512),
                 pl.BlockSpec((None, 512, tm), lambda i: (i // ns, 0, i % ns)),
                 tok(GLA_QK), tok(GLA_QK), tok(GLA_WIDTH), tok(GLA_WIDTH), tok(GLA_QK), tok(GLA_QK)]
    return pl.pallas_call(
        _proj_kernel, grid=(M // tm,), in_specs=in_specs, out_specs=out_specs, out_shape=out_shape,
        compiler_params=pltpu.CompilerParams(dimension_semantics=("parallel",),
                                             vmem_limit_bytes=VMEM_LIMIT),
        name="proj")(*ins)


def _attn_kernel(lamv_ref, g_ref, q_ref, k_ref, vt_ref, o_ref, s0_ref, s1_ref, p0_ref, p1_ref,
                 *, seq, lam_init):
    tq, tk, rc = ATT_TQ, ATT_TK, ATT_RC
    ncol = 2 * tq
    nch = tk // rc
    n = seq // tk
    nq = seq // tq
    s_bufs = (s0_ref, s1_ref)
    p_bufs = (p0_ref, p1_ref)

    lv = lamv_ref[...]
    lam = (jnp.exp(jnp.sum(lv[0:1] * lv[1:2], keepdims=True))
           - jnp.exp(jnp.sum(lv[2:3] * lv[3:4], keepdims=True)) + lam_init)
    gain = g_ref[...] * (1.0 - lam_init)

    def make_qq(qi):
        qt = q_ref[pl.ds(pl.multiple_of(qi * tq, tq), tq), :].astype(F32).T
        row = lax.broadcasted_iota(jnp.int32, qt.shape, 0)
        return jnp.concatenate([jnp.where(row < DA_DK, qt, 0.0),
                                jnp.where(row >= DA_DK, qt, 0.0)], axis=1).astype(BF16)

    def qk_stage(qq, t):
        sc = _dot(k_ref[t * tk:(t + 1) * tk, :], qq)
        s_bufs[t % 2][...] = sc
        return jnp.max(sc.reshape(tk // 8, 8, ncol), axis=0)

    def softmax_stage(t, mx, m, l):
        m_new = jnp.maximum(m, jnp.max(mx, axis=0, keepdims=True))
        alpha = jnp.exp2(m - m_new)
        mb = jnp.broadcast_to(m_new, (8, ncol))[None]
        ls = None
        for c in range(nch):
            rows = slice(c * rc, (c + 1) * rc)
            p = jnp.exp2(s_bufs[t % 2][rows, :].reshape(rc // 8, 8, ncol) - mb)
            part = jnp.sum(p, axis=0)
            ls = part if ls is None else ls + part
            p_bufs[t % 2][rows, :] = p.reshape(rc, ncol).astype(BF16)
        return m_new, alpha * l + jnp.sum(ls, axis=0, keepdims=True), alpha

    def pv_stage(t, alpha, acc):
        return alpha * acc + _dot(vt_ref[:, t * tk:(t + 1) * tk], p_bufs[t % 2][...])

    def finalize(qi, l, acc):
        on = acc / l
        o = on[:, :tq] - lam * on[:, tq:]
        ms = jnp.mean(o * o, axis=0, keepdims=True)
        y = o * lax.rsqrt(ms + RMS_EPS) * gain
        o_ref[pl.ds(pl.multiple_of(qi * tq, tq), tq), :] = y.T.astype(BF16)

    m_init = jnp.full((1, ncol), -jnp.inf, F32)
    l_init = jnp.zeros((1, ncol), F32)
    acc_init = jnp.zeros((DA_DV, ncol), F32)

    def query_tile(qi, qq, state, first):
        mx, m, l, alpha, acc = state
        for t in range(n):
            mx_new = qk_stage(qq, t)
            if t == 0 and not first:
                acc = pv_stage(n - 2, alpha, acc)
                m, l, alpha = softmax_stage(n - 1, mx, m, l)
            elif t == 1:
                if not first:
                    acc = pv_stage(n - 1, alpha, acc)
                    finalize(qi - 1, l, acc)
                m, l, alpha = softmax_stage(0, mx, m_init, l_init)
                acc = acc_init
            elif t >= 2:
                acc = pv_stage(t - 2, alpha, acc)
                m, l, alpha = softmax_stage(t - 1, mx, m, l)
            mx = mx_new
        return mx, m, l, alpha, acc

    state = (jnp.zeros((8, ncol), F32), m_init, l_init, jnp.ones((1, ncol), F32), acc_init)
    state = query_tile(0, make_qq(0), state, True)

    def body(qi, carry):
        qq, state = carry
        qq_next = make_qq(jnp.minimum(qi + 1, nq - 1))
        return qq_next, query_tile(qi, qq, state, False)

    _, state = lax.fori_loop(1, nq, body, (make_qq(1), state))
    mx, m, l, alpha, acc = state
    acc = pv_stage(n - 2, alpha, acc)
    m, l, alpha = softmax_stage(n - 1, mx, m, l)
    acc = pv_stage(n - 1, alpha, acc)
    finalize(nq - 1, l, acc)


def _attn_call(q, k, vt, lamv, subg, lam_init):
    B, S, _ = q.shape
    seq_blk = pl.BlockSpec((None, S, LANES), lambda b, h: (b, 0, h))
    return pl.pallas_call(
        functools.partial(_attn_kernel, seq=S, lam_init=lam_init),
        grid=(B, DA_HEADS),
        in_specs=[
            pl.BlockSpec(lamv.shape, lambda b, h: (0, 0)),
            pl.BlockSpec(subg.shape, lambda b, h: (0, 0)),
            seq_blk, seq_blk,
            pl.BlockSpec((None, LANES, S), lambda b, h: (b, h, 0)),
        ],
        out_specs=seq_blk,
        out_shape=jax.ShapeDtypeStruct((B, S, DA_WIDTH), BF16),
        scratch_shapes=[pltpu.VMEM((ATT_TK, 2 * ATT_TQ), F32)] * 2
        + [pltpu.VMEM((ATT_TK, 2 * ATT_TQ), BF16)] * 2,
        compiler_params=pltpu.CompilerParams(
            dimension_semantics=("parallel", "parallel"),
            vmem_limit_bytes=VMEM_LIMIT),
        name="diff_attn")(lamv, subg, q, k, vt)


GLA_MXU_LEVELS = (4, 2, 1)


def _split2(g):
    g1 = g.astype(BF16)
    return g1, (g - g1.astype(F32)).astype(BF16)


def _gla_tables(C):
    t = np.arange(C)
    tri_i = (t[None, :] <= t[:, None]).astype(np.float32)
    tri_s = (t[None, :] < t[:, None]).astype(np.float32)
    blocks = []
    for w in GLA_MXU_LEVELS:
        rho = (t // (2 * w)) * 2 * w + w
        odd = ((t & w) != 0).astype(np.float32)[:, None]
        d_f = tri_i - tri_i[rho]
        d_b = tri_s - tri_s[rho]
        blocks.append(np.concatenate([odd * d_f, (odd - 1.0) * d_b], 1))
        blocks.append(np.concatenate([(odd - 1.0) * d_f, odd * d_b], 1))
    tab = np.concatenate(blocks, 0)
    x = t[:, None] ^ t[None, :]
    hb = np.where(x > 0, 2 ** np.floor(np.log2(np.maximum(x, 1))), 0).astype(np.int32)
    tri = np.concatenate([tri_i, tri_s], 0)
    return (jnp.asarray(np.concatenate([tab, tab], 1), BF16),
            jnp.asarray(np.concatenate([tri, tri], 1), BF16), jnp.asarray(hb))


def _pair_ref_rows(c, w):
    C, n = c.shape
    c3 = c.reshape(C // (2 * w), 2 * w, n)
    return jnp.broadcast_to(c3[:, w:w + 1, :], c3.shape).reshape(C, n)


def _gla_kernel(tab_ref, tri_ref, hb_ref, q_ref, k_ref, v_ref, gf_ref, gb_ref,
                qb_ref, kb_ref, vb_ref, gbb_ref, om_ref, ob_ref, sf_ref, sb_ref):
    @pl.when(pl.program_id(1) == 0)
    def _():
        sf_ref[...] = jnp.zeros_like(sf_ref)
        sb_ref[...] = jnp.zeros_like(sb_ref)

    chunks = []
    for j in range(GLA_SUB):
        rf = pl.ds(j * GLA_C, GLA_C)
        rb = pl.ds((GLA_SUB - 1 - j) * GLA_C, GLA_C)
        chunks.append(_gla_chunk(tab_ref, tri_ref, hb_ref, q_ref.at[rf], k_ref.at[rf], v_ref.at[rf],
                                 gf_ref.at[rf], gb_ref.at[rf], qb_ref.at[rb], kb_ref.at[rb],
                                 vb_ref.at[rb], gbb_ref.at[rb], om_ref.at[rf], ob_ref.at[rb],
                                 sf_ref, sb_ref))
    for _ in range(GLA_PHASES):
        for chunk in chunks:
            next(chunk)


GLA_PHASES = 3


def _gla_chunk(tab_ref, tri_ref, hb_ref, q_ref, k_ref, v_ref, gf_ref, gb_ref,
               qb_ref, kb_ref, vb_ref, gbb_ref, om_ref, ob_ref, sf_ref, sb_ref):
    C = GLA_C
    hb = hb_ref[...]
    rowi = lax.broadcasted_iota(jnp.int32, (C, 1), 0)
    lane = lax.broadcasted_iota(jnp.int32, (C, LANES), 1)
    head_lo = lane < GLA_DK
    sr = lax.broadcasted_iota(jnp.int32, (2 * GLA_DV, 2 * GLA_DK), 0)
    sc = lax.broadcasted_iota(jnp.int32, (2 * GLA_DV, 2 * GLA_DK), 1)
    sdiag = (sr < GLA_DV) == (sc < GLA_DK)

    q2 = qb_ref[...].astype(F32)
    k2 = kb_ref[...].astype(F32)
    v2 = vb_ref[...]
    g2 = gbb_ref[...]
    cx = _dot(tri_ref[C:2 * C, :], jnp.concatenate(_split2(g2), axis=0))
    tot = cx[C - 1:C, :] + g2[C - 1:C, :]
    qe2 = (q2 * jnp.exp(tot - cx)).astype(BF16)
    ke2 = (k2 * jnp.exp(cx)).astype(BF16)
    dec2 = jnp.exp(tot)

    q = q_ref[...].astype(F32)
    k = k_ref[...].astype(F32)
    v = v_ref[...]
    gf1, gf2 = _split2(gf_ref[...])
    gb1, gb2 = _split2(gb_ref[...])
    cf = _dot(tri_ref[0:C, :], jnp.concatenate([gf1, gf2], axis=0))
    cb = _dot(tri_ref[C:2 * C, :], jnp.concatenate([gb1, gb2], axis=0))
    g4 = jnp.concatenate([gf1, gb1, gf2, gb2], axis=0)

    levels = [C >> (i + 1) for i in range(C.bit_length() - 1)]
    operands = []
    for w in levels:
        if w in GLA_MXU_LEVELS:
            base = 2 * C * GLA_MXU_LEVELS.index(w)
            arg = _dot(tab_ref[base:base + 2 * C, :], g4)
            arg_q, arg_k = arg[:C], arg[C:]
        else:
            odd = (rowi & w) != 0
            d_f = cf - _pair_ref_rows(cf, w)
            d_b = cb - _pair_ref_rows(cb, w)
            arg_q = jnp.where(odd, d_f, -d_b)
            arg_k = jnp.where(odd, d_b, -d_f)
        operands.append(((q * jnp.exp(arg_q)).astype(BF16), (k * jnp.exp(arg_k)).astype(BF16)))
    operands.append((q_ref[...], k_ref[...]))
    cf_last = cf[C - 1:C, :]
    qe = (q * jnp.exp(cf)).astype(BF16)
    ke = (k * jnp.exp(cf_last - cf)).astype(BF16)
    dec = jnp.exp(cf_last)
    yield

    def head_blocks(qt, kt):
        out = []
        for h in range(GLA_HEADS):
            pr = slice((h // 2) * LANES, (h // 2 + 1) * LANES)
            keep = head_lo if h % 2 == 0 else jnp.logical_not(head_lo)
            out.append(_dot_nt(jnp.where(keep, qt[:, pr], jnp.zeros((), BF16)), kt[:, pr]))
        return out

    blocks = [head_blocks(qt, kt) for qt, kt in operands]
    yield

    a = [2.0 * blocks[-1][h] for h in range(GLA_HEADS)]
    for w, blk in zip(levels, blocks[:-1]):
        sel = hb == w
        a = [jnp.where(sel, blk[h], a[h]) for h in range(GLA_HEADS)]

    outs = []
    for p in range(2):
        pr = slice(p * LANES, (p + 1) * LANES)
        vr = slice(p * 2 * GLA_DV, (p + 1) * 2 * GLA_DV)
        st = sb_ref[p]
        outs.append(_dot_nt(qe2[:, pr], st.astype(BF16)))
        upd = _dot_tn(v2[:, vr], ke2[:, pr])
        sb_ref[p] = dec2[:, pr] * st + jnp.where(sdiag, upd, 0.0)
    ob_ref[...] = jnp.concatenate(outs, axis=1)

    outs = []
    for p in range(2):
        pr = slice(p * LANES, (p + 1) * LANES)
        vr = slice(p * 2 * GLA_DV, (p + 1) * 2 * GLA_DV)
        st = sf_ref[p]
        inter = _dot_nt(qe[:, pr], st.astype(BF16))
        for hh in range(2):
            h = 2 * p + hh
            vh = slice(h * GLA_DV, (h + 1) * GLA_DV)
            outs.append(_dot(a[h].astype(BF16), v[:, vh]) + inter[:, hh * GLA_DV:(hh + 1) * GLA_DV])
        upd = _dot_tn(v[:, vr], ke[:, pr])
        sf_ref[p] = dec[:, pr] * st + jnp.where(sdiag, upd, 0.0)
    om_ref[...] = jnp.concatenate(outs, axis=1)
    yield


def _gla_call(gq, gk, gv, gf, gb, B, S):
    M = gq.shape[0]
    rows = GLA_SUB * GLA_C
    n = S // rows
    fw = lambda width: pl.BlockSpec((rows, width), lambda b, i: (b * n + i, 0))
    bw = lambda width: pl.BlockSpec((rows, width), lambda b, i: (b * n + n - 1 - i, 0))
    tables = _gla_tables(GLA_C)
    return pl.pallas_call(
        _gla_kernel, grid=(B, n),
        in_specs=[pl.BlockSpec(t.shape, lambda b, i: (0, 0)) for t in tables]
        + [fw(GLA_QK), fw(GLA_QK), fw(GLA_WIDTH), fw(GLA_QK), fw(GLA_QK),
           bw(GLA_QK), bw(GLA_QK), bw(GLA_WIDTH), bw(GLA_QK)],
        out_specs=[fw(GLA_WIDTH), bw(GLA_WIDTH)],
        out_shape=[jax.ShapeDtypeStruct((M, GLA_WIDTH), F32)] * 2,
        scratch_shapes=[pltpu.VMEM((2, 2 * GLA_DV, 2 * GLA_DK), F32)] * 2,
        compiler_params=pltpu.CompilerParams(dimension_semantics=("parallel", "arbitrary"),
                                             vmem_limit_bytes=VMEM_LIMIT),
        name="gla")(*tables, gq, gk, gv, gf, gb, gq, gk, gv, gb)


def _layer_norm(h, g, b):
    mu = jnp.mean(h, axis=-1, keepdims=True)
    d = h - mu
    var = jnp.mean(d * d, axis=-1, keepdims=True)
    return d * lax.rsqrt(var + LN_EPS) * g + b


def _lockstep(parts):
    parts = list(parts)
    while parts:
        parts = [p for p in parts if next(p, StopIteration) is not StopIteration]


def _row_groups(ref, count):
    rows = ref.shape[0] // count
    return [ref.at[pl.ds(i * rows, rows)] for i in range(count)]


def _merge_rows(x_ref, ya_ref, om_ref, ob_ref, rs_ref, wg_ref, bg_ref, gn_ref,
                wa_ref, wb_ref, wo_ref, lg_ref, lb_ref, o_ref, alpha):
    x = x_ref[...]
    gate_logits = _dot(x.astype(BF16), wg_ref[...]) + bg_ref[...]
    pa = _dot(ya_ref[...], wa_ref[...])
    o = om_ref[...] + ob_ref[...]
    gn = gn_ref[...]
    cols = []
    for h in range(GLA_HEADS):
        oh = o[:, h * GLA_DV:(h + 1) * GLA_DV]
        ms = jnp.mean(oh * oh, axis=-1, keepdims=True)
        cols.append(oh * lax.rsqrt(ms + RMS_EPS) * gn)
    yb = (jnp.concatenate(cols, axis=1) * rs_ref[...].astype(F32)).astype(BF16)
    yield
    pb = _dot(yb, wb_ref[...])
    gates = jax.nn.sigmoid(gate_logits)
    mixin = (gates[:, :D_MODEL] * pa + gates[:, D_MODEL:] * pb).astype(BF16)
    yield
    mix = _dot(mixin, wo_ref[...])
    yield
    o_ref[...] = _layer_norm(alpha * x + mix, lg_ref[...], lb_ref[...])


def _merge_kernel(x_ref, ya_ref, om_ref, ob_ref, rs_ref, wg_ref, bg_ref, gn_ref,
                  wa_ref, wb_ref, wo_ref, lg_ref, lb_ref, o_ref, *, alpha):
    tiles = [_row_groups(r, ROW_GROUPS) for r in (x_ref, ya_ref, om_ref, ob_ref, rs_ref, o_ref)]
    _lockstep(_merge_rows(x, ya, om, ob, rs, wg_ref, bg_ref, gn_ref, wa_ref, wb_ref, wo_ref,
                          lg_ref, lb_ref, o, alpha) for x, ya, om, ob, rs, o in zip(*tiles))


def _merge_call(x2, ya, om, ob, rs, w, alpha):
    M = x2.shape[0]
    tm = MERGE_TM
    full = lambda a: pl.BlockSpec(a.shape, lambda i: (0,) * a.ndim)
    tok = lambda n: pl.BlockSpec((tm, n), lambda i: (i, 0))
    ws = [w["wgate"], w["bgate"], w["gnorm"], w["wbra"], w["wbrb"], w["wout"], w["ln1g"], w["ln1b"]]
    return pl.pallas_call(
        functools.partial(_merge_kernel, alpha=alpha), grid=(M // tm,),
        in_specs=[tok(D_MODEL), tok(512), tok(512), tok(512), tok(512)] + [full(a) for a in ws],
        out_specs=tok(D_MODEL), out_shape=jax.ShapeDtypeStruct((M, D_MODEL), F32),
        compiler_params=pltpu.CompilerParams(dimension_semantics=("parallel",),
                                             vmem_limit_bytes=VMEM_LIMIT),
        name="merge")(x2, ya, om, ob, rs, *ws)


def _mlp_kernel(x_ref, w1_ref, w2_ref, lg_ref, lb_ref, o_ref, *, alpha):
    x = x_ref[...]
    h = jnp.maximum(_dot(x.astype(BF16), w1_ref[...]), 0.0)
    h = _dot((h * h).astype(BF16), w2_ref[...])
    o_ref[...] = _layer_norm(alpha * x + h, lg_ref[...], lb_ref[...])


def _mlp_call(x1, w, alpha):
    M = x1.shape[0]
    tm = MLP_TM
    once = lambda a: pl.BlockSpec(a.shape, lambda i: (0,) * a.ndim, pipeline_mode=pl.Buffered(1))
    tok = pl.BlockSpec((tm, D_MODEL), lambda i: (i, 0))
    ws = [w["w1"], w["w2"], w["ln2g"], w["ln2b"]]
    return pl.pallas_call(
        functools.partial(_mlp_kernel, alpha=alpha), grid=(M // tm,),
        in_specs=[tok] + [once(a) for a in ws],
        out_specs=tok, out_shape=jax.ShapeDtypeStruct((M, D_MODEL), F32),
        compiler_params=pltpu.CompilerParams(dimension_semantics=("parallel",),
                                             vmem_limit_bytes=VMEM_LIMIT),
        name="mlp")(x1, *ws)


def _rope_tables(S):
    half = ROT_DIM // 2
    inv = 1.0 / (ROPE_THETA ** (jnp.arange(0, ROT_DIM, 2, dtype=F32) / ROT_DIM))
    ang = jnp.arange(S, dtype=F32)[:, None] * inv[None, :]
    cos, sin = jnp.cos(ang), jnp.sin(ang)
    d = jnp.arange(LANES) % DA_DK
    idx = d % half
    cos_t = jnp.where(d[None, :] < ROT_DIM, cos[:, idx], 1.0)
    sin_a = jnp.where(d[None, :] < half, -sin[:, idx], 0.0)
    sin_b = jnp.where((d[None, :] >= half) & (d[None, :] < ROT_DIM), sin[:, idx], 0.0)
    return cos_t.astype(F32), sin_a.astype(F32), sin_b.astype(F32)


def _layer_weights(l, w_in, subln_g, w_dec_f, b_dec_f, w_dec_b, b_dec_b, gla_norm_g, w_br_a, w_br_b,
                   b_gate, w_out, ln1_g, ln1_b, w_mlp1, w_mlp2, ln2_g, ln2_b):
    offs = [0]
    for s in IN_SIZES:
        offs.append(offs[-1] + s)
    col = lambda i: w_in[l][:, offs[i]:offs[i + 1]]
    zeros = jnp.zeros((GLA_RANK, GLA_QK), F32)
    row = lambda a: a[l].reshape(1, -1).astype(F32)
    return {
        "wq": (col(0) * (DA_DK ** -0.5 * LOG2E)).astype(BF16), "wk": col(1).astype(BF16), "wv": col(2).astype(BF16),
        "wgq": (col(3) * GLA_DK ** -0.5).astype(BF16), "wgk": col(4).astype(BF16),
        "wgv": col(5).astype(BF16), "wgr": col(6).astype(BF16),
        "wz": jnp.concatenate([col(7), col(8)], axis=1).astype(BF16),
        "wdec": jnp.concatenate([jnp.concatenate([w_dec_f[l], zeros], axis=1),
                                 jnp.concatenate([zeros, w_dec_b[l]], axis=1)], axis=0).astype(BF16),
        "bdec": jnp.concatenate([b_dec_f[l], b_dec_b[l]]).reshape(1, -1).astype(F32),
        "wgate": col(9).astype(BF16), "bgate": row(b_gate),
        "subg": subln_g[l].reshape(-1, 1).astype(F32), "gnorm": row(gla_norm_g),
        "wbra": w_br_a[l].astype(BF16), "wbrb": w_br_b[l].astype(BF16), "wout": w_out[l].astype(BF16),
        "ln1g": row(ln1_g), "ln1b": row(ln1_b),
        "w1": w_mlp1[l].astype(BF16), "w2": w_mlp2[l].astype(BF16),
        "ln2g": row(ln2_g), "ln2b": row(ln2_b),
    }


def kernel(x_prompt, x_sample, w_in, lam_q1, lam_k1, lam_q2, lam_k2, subln_g, w_dec_f, b_dec_f, w_dec_b, b_dec_b, gla_norm_g, w_br_a, w_br_b, b_gate, w_out, ln1_g, ln1_b, w_mlp1, w_mlp2, ln2_g, ln2_b):
    depth = w_in.shape[0]
    alpha = (2 * depth) ** 0.25
    layers = []
    for l in range(depth):
        w = _layer_weights(l, w_in, subln_g, w_dec_f, b_dec_f, w_dec_b, b_dec_b, gla_norm_g,
                           w_br_a, w_br_b, b_gate, w_out, ln1_g, ln1_b, w_mlp1, w_mlp2, ln2_g, ln2_b)
        lamv = jnp.stack([lam_q1[l], lam_k1[l], lam_q2[l], lam_k2[l]]).astype(F32)
        layers.append((w, lamv, 0.8 - 0.6 * math.exp(-0.3 * l)))

    def trunk(x):
        B, S, D = x.shape
        tabs = _rope_tables(S)
        x2 = x.reshape(B * S, D)
        for w, lamv, lam_init in layers:
            q, k, vt, gq, gk, gv, rs, gf, gb = _proj_call(x2, B, S, w, tabs)
            ya = _attn_call(q.reshape(B, S, 512), k.reshape(B, S, 512), vt, lamv, w["subg"], lam_init)
            om, ob = _gla_call(gq, gk, gv, gf, gb, B, S)
            x1 = _merge_call(x2, ya.reshape(B * S, 512), om, ob, rs, w, alpha)
            x2 = _mlp_call(x1, w, alpha)
        return x2.reshape(B, S, D)

    return (trunk(x_prompt), trunk(x_sample))
```

```python
import functools
import math

import jax
import jax.numpy as jnp
import numpy as np
from jax import lax
from jax.experimental import pallas as pl
from jax.experimental.pallas import tpu as pltpu

F32 = jnp.float32
BF16 = jnp.bfloat16

D_MODEL = 1024
DA_HEADS = 4
DA_DK = 64
DA_DV = 128
DA_WIDTH = 512
ROT_DIM = 16
ROPE_THETA = 500000.0
GLA_HEADS = 4
GLA_DK = 64
GLA_DV = 128
GLA_QK = 256
GLA_WIDTH = 512
GLA_RANK = 16
GLA_TAU = 16.0
D_FF = 4096
LN_EPS = 1e-5
RMS_EPS = 1e-6
IN_SIZES = (512, 512, 512, 256, 256, 512, 512, 16, 16, 2048)

LANES = 128
VMEM_LIMIT = 56 * 1024 * 1024

PROJ_TM = 512
ATT_TQ = 128
ATT_TK = 1024
ATT_RC = 128
ATT_SUM_ROWS = 16
LOG2E = math.log2(math.e)
GLA_C = 128
GLA_SUB = 4
MERGE_TM = 512
MLP_TM = 512
ROW_GROUPS = 2


def _dot(a, b):
    return jnp.dot(a, b, preferred_element_type=F32)


def _dot_nt(a, b):
    return lax.dot_general(a, b, (((1,), (1,)), ((), ())), preferred_element_type=F32)


def _dot_tn(a, b):
    return lax.dot_general(a, b, (((0,), (0,)), ((), ())), preferred_element_type=F32)


def _proj_kernel(x_ref, wq_ref, wk_ref, wv_ref, wgq_ref, wgk_ref, wgv_ref, wgr_ref, wz_ref,
                 wdec_ref, bdec_ref, cos_ref, sa_ref, sb_ref,
                 q_ref, k_ref, vt_ref, gq_ref, gk_ref, gv_ref, rs_ref, gf_ref, gb_ref):
    x = x_ref[...].astype(BF16)
    cos = cos_ref[...]
    sin_a = sa_ref[...]
    sin_b = sb_ref[...]

    def rope(t):
        cols = []
        for j in range(t.shape[1] // LANES):
            blk = t[:, j * LANES:(j + 1) * LANES]
            cols.append(blk * cos + pltpu.roll(blk, LANES - ROT_DIM // 2, 1) * sin_a
                        + pltpu.roll(blk, ROT_DIM // 2, 1) * sin_b)
        return jnp.concatenate(cols, axis=1)

    z = _dot(x, wz_ref[...])
    zz = _dot(z.astype(BF16), wdec_ref[...]) + bdec_ref[...]
    g = -(jnp.maximum(-zz, 0.0) + jnp.log1p(jnp.exp(-jnp.abs(zz)))) / GLA_TAU
    gf_ref[...] = g[:, :GLA_QK]
    gb_ref[...] = g[:, GLA_QK:]
    r = _dot(x, wgr_ref[...])
    rs_ref[...] = (r * jax.nn.sigmoid(r)).astype(BF16)
    q_ref[...] = rope(_dot(x, wq_ref[...])).astype(BF16)
    k_ref[...] = rope(_dot(x, wk_ref[...])).astype(BF16)
    vt_ref[...] = _dot(x, wv_ref[...]).T.astype(BF16)
    gq_ref[...] = _dot(x, wgq_ref[...]).astype(BF16)
    gk_ref[...] = _dot(x, wgk_ref[...]).astype(BF16)
    gv_ref[...] = _dot(x, wgv_ref[...]).astype(BF16)


def _proj_call(x2, B, S, w, tabs):
    M = x2.shape[0]
    tm = PROJ_TM
    ns = S // tm
    full = lambda a: pl.BlockSpec(a.shape, lambda i: (0,) * a.ndim)
    tok = lambda n: pl.BlockSpec((tm, n), lambda i: (i, 0))
    tab = pl.BlockSpec((tm, LANES), lambda i: (i % ns, 0))
    ins = [x2, w["wq"], w["wk"], w["wv"], w["wgq"], w["wgk"], w["wgv"], w["wgr"], w["wz"],
           w["wdec"], w["bdec"], tabs[0], tabs[1], tabs[2]]
    in_specs = [tok(D_MODEL)] + [full(a) for a in ins[1:11]] + [tab, tab, tab]
    out_shape = [
        jax.ShapeDtypeStruct((M, 512), BF16),
        jax.ShapeDtypeStruct((M, 512), BF16),
        jax.ShapeDtypeStruct((B, 512, S), BF16),
        jax.ShapeDtypeStruct((M, GLA_QK), BF16),
        jax.ShapeDtypeStruct((M, GLA_QK), BF16),
        jax.ShapeDtypeStruct((M, GLA_WIDTH), BF16),
        jax.ShapeDtypeStruct((M, GLA_WIDTH), BF16),
        jax.ShapeDtypeStruct((M, GLA_QK), F32),
        jax.ShapeDtypeStruct((M, GLA_QK), F32),
    ]
    out_specs = [tok(512), tok(512),
                 pl.BlockSpec((None, 512, tm), lambda i: (i // ns, 0, i % ns)),
                 tok(GLA_QK), tok(GLA_QK), tok(GLA_WIDTH), tok(GLA_WIDTH), tok(GLA_QK), tok(GLA_QK)]
    return pl.pallas_call(
        _proj_kernel, grid=(M // tm,), in_specs=in_specs, out_specs=out_specs, out_shape=out_shape,
        compiler_params=pltpu.CompilerParams(dimension_semantics=("parallel",),
                                             vmem_limit_bytes=VMEM_LIMIT),
        name="proj")(*ins)


def _attn_kernel(lamv_ref, g_ref, q_ref, k_ref, vt_ref, o_ref, s0_ref, s1_ref, p0_ref, p1_ref,
                 *, seq, lam_init):
    tq, tk, rc = ATT_TQ, ATT_TK, ATT_RC
    ncol = 2 * tq
    nch = tk // rc
    n = seq // tk
    nq = seq // tq
    s_bufs = (s0_ref, s1_ref)
    p_bufs = (p0_ref, p1_ref)

    lv = lamv_ref[...]
    lam = (jnp.exp(jnp.sum(lv[0:1] * lv[1:2], keepdims=True))
           - jnp.exp(jnp.sum(lv[2:3] * lv[3:4], keepdims=True)) + lam_init)
    gain = g_ref[...] * (1.0 - lam_init)

    def make_qq(qi):
        qt = q_ref[pl.ds(pl.multiple_of(qi * tq, tq), tq), :].astype(F32).T
        row = lax.broadcasted_iota(jnp.int32, qt.shape, 0)
        return jnp.concatenate([jnp.where(row < DA_DK, qt, 0.0),
                                jnp.where(row >= DA_DK, qt, 0.0)], axis=1).astype(BF16)

    def qk_stage(qq, t):
        sc = _dot(k_ref[t * tk:(t + 1) * tk, :], qq)
        s_bufs[t % 2][...] = sc
        return jnp.max(sc.reshape(tk // 8, 8, ncol), axis=0)

    def softmax_stage(t, mx, m):
        m_new = jnp.maximum(m, jnp.max(mx, axis=0, keepdims=True))
        alpha = jnp.exp2(m - m_new)
        mb = jnp.broadcast_to(m_new, (8, ncol))[None]
        for c in range(nch):
            rows = slice(c * rc, (c + 1) * rc)
            p = jnp.exp2(s_bufs[t % 2][rows, :].reshape(rc // 8, 8, ncol) - mb)
            p_bufs[t % 2][rows, :] = p.reshape(rc, ncol).astype(BF16)
        return m_new, alpha

    ones_rows = jnp.ones((ATT_SUM_ROWS, tk), BF16)

    def pv_stage(t, alpha, acc):
        lhs = jnp.concatenate([vt_ref[:, t * tk:(t + 1) * tk], ones_rows], axis=0)
        return alpha * acc + _dot(lhs, p_bufs[t % 2][...])

    def finalize(qi, acc):
        on = acc[:DA_DV] / acc[DA_DV:DA_DV + 1]
        o = on[:, :tq] - lam * on[:, tq:]
        ms = jnp.mean(o * o, axis=0, keepdims=True)
        y = o * lax.rsqrt(ms + RMS_EPS) * gain
        o_ref[pl.ds(pl.multiple_of(qi * tq, tq), tq), :] = y.T.astype(BF16)

    m_init = jnp.full((1, ncol), -jnp.inf, F32)
    acc_init = jnp.zeros((DA_DV + ATT_SUM_ROWS, ncol), F32)

    def query_tile(qi, qq, state, first):
        mx, m, alpha, acc = state
        for t in range(n):
            mx_new = qk_stage(qq, t)
            if t == 0 and not first:
                acc = pv_stage(n - 2, alpha, acc)
                m, alpha = softmax_stage(n - 1, mx, m)
            elif t == 1:
                if not first:
                    acc = pv_stage(n - 1, alpha, acc)
                    finalize(qi - 1, acc)
                m, alpha = softmax_stage(0, mx, m_init)
                acc = acc_init
            elif t >= 2:
                acc = pv_stage(t - 2, alpha, acc)
                m, alpha = softmax_stage(t - 1, mx, m)
            mx = mx_new
        return mx, m, alpha, acc

    state = (jnp.zeros((8, ncol), F32), m_init, jnp.ones((1, ncol), F32), acc_init)
    state = query_tile(0, make_qq(0), state, True)

    def body(qi, carry):
        qq, state = carry
        qq_next = make_qq(jnp.minimum(qi + 1, nq - 1))
        return qq_next, query_tile(qi, qq, state, False)

    _, state = lax.fori_loop(1, nq, body, (make_qq(1), state))
    mx, m, alpha, acc = state
    acc = pv_stage(n - 2, alpha, acc)
    m, alpha = softmax_stage(n - 1, mx, m)
    acc = pv_stage(n - 1, alpha, acc)
    finalize(nq - 1, acc)


def _attn_call(q, k, vt, lamv, subg, lam_init):
    B, S, _ = q.shape
    seq_blk = pl.BlockSpec((None, S, LANES), lambda b, h: (b, 0, h))
    return pl.pallas_call(
        functools.partial(_attn_kernel, seq=S, lam_init=lam_init),
        grid=(B, DA_HEADS),
        in_specs=[
            pl.BlockSpec(lamv.shape, lambda b, h: (0, 0)),
            pl.BlockSpec(subg.shape, lambda b, h: (0, 0)),
            seq_blk, seq_blk,
            pl.BlockSpec((None, LANES, S), lambda b, h: (b, h, 0)),
        ],
        out_specs=seq_blk,
        out_shape=jax.ShapeDtypeStruct((B, S, DA_WIDTH), BF16),
        scratch_shapes=[pltpu.VMEM((ATT_TK, 2 * ATT_TQ), F32)] * 2
        + [pltpu.VMEM((ATT_TK, 2 * ATT_TQ), BF16)] * 2,
        compiler_params=pltpu.CompilerParams(
            dimension_semantics=("parallel", "parallel"),
            vmem_limit_bytes=VMEM_LIMIT),
        name="diff_attn")(lamv, subg, q, k, vt)


GLA_MXU_LEVELS = (4, 2, 1)


def _split2(g):
    g1 = g.astype(BF16)
    return g1, (g - g1.astype(F32)).astype(BF16)


def _gla_tables(C):
    t = np.arange(C)
    tri_i = (t[None, :] <= t[:, None]).astype(np.float32)
    tri_s = (t[None, :] < t[:, None]).astype(np.float32)
    blocks = []
    for w in GLA_MXU_LEVELS:
        rho = (t // (2 * w)) * 2 * w + w
        odd = ((t & w) != 0).astype(np.float32)[:, None]
        d_f = tri_i - tri_i[rho]
        d_b = tri_s - tri_s[rho]
        blocks.append(np.concatenate([odd * d_f, (odd - 1.0) * d_b], 1))
        blocks.append(np.concatenate([(odd - 1.0) * d_f, odd * d_b], 1))
    tab = np.concatenate(blocks, 0)
    x = t[:, None] ^ t[None, :]
    hb = np.where(x > 0, 2 ** np.floor(np.log2(np.maximum(x, 1))), 0).astype(np.int32)
    tri = np.concatenate([tri_i, tri_s], 0)
    return (jnp.asarray(np.concatenate([tab, tab], 1), BF16),
            jnp.asarray(np.concatenate([tri, tri], 1), BF16), jnp.asarray(hb))


def _pair_ref_rows(c, w):
    C, n = c.shape
    c3 = c.reshape(C // (2 * w), 2 * w, n)
    return jnp.broadcast_to(c3[:, w:w + 1, :], c3.shape).reshape(C, n)


def _gla_kernel(tab_ref, tri_ref, hb_ref, q_ref, k_ref, v_ref, gf_ref, gb_ref,
                qb_ref, kb_ref, vb_ref, gbb_ref, om_ref, ob_ref, sf_ref, sb_ref):
    @pl.when(pl.program_id(1) == 0)
    def _():
        sf_ref[...] = jnp.zeros_like(sf_ref)
        sb_ref[...] = jnp.zeros_like(sb_ref)

    chunks = []
    for j in range(GLA_SUB):
        rf = pl.ds(j * GLA_C, GLA_C)
        rb = pl.ds((GLA_SUB - 1 - j) * GLA_C, GLA_C)
        chunks.append(_gla_chunk(tab_ref, tri_ref, hb_ref, q_ref.at[rf], k_ref.at[rf], v_ref.at[rf],
                                 gf_ref.at[rf], gb_ref.at[rf], qb_ref.at[rb], kb_ref.at[rb],
                                 vb_ref.at[rb], gbb_ref.at[rb], om_ref.at[rf], ob_ref.at[rb],
                                 sf_ref, sb_ref))
    for _ in range(GLA_PHASES):
        for chunk in chunks:
            next(chunk)


GLA_PHASES = 3


def _gla_chunk(tab_ref, tri_ref, hb_ref, q_ref, k_ref, v_ref, gf_ref, gb_ref,
               qb_ref, kb_ref, vb_ref, gbb_ref, om_ref, ob_ref, sf_ref, sb_ref):
    C = GLA_C
    hb = hb_ref[...]
    rowi = lax.broadcasted_iota(jnp.int32, (C, 1), 0)
    lane = lax.broadcasted_iota(jnp.int32, (C, LANES), 1)
    head_lo = lane < GLA_DK
    sr = lax.broadcasted_iota(jnp.int32, (2 * GLA_DV, 2 * GLA_DK), 0)
    sc = lax.broadcasted_iota(jnp.int32, (2 * GLA_DV, 2 * GLA_DK), 1)
    sdiag = (sr < GLA_DV) == (sc < GLA_DK)

    q2 = qb_ref[...].astype(F32)
    k2 = kb_ref[...].astype(F32)
    v2 = vb_ref[...]
    g2 = gbb_ref[...]
    cx = _dot(tri_ref[C:2 * C, :], jnp.concatenate(_split2(g2), axis=0))
    tot = cx[C - 1:C, :] + g2[C - 1:C, :]
    qe2 = (q2 * jnp.exp(tot - cx)).astype(BF16)
    ke2 = (k2 * jnp.exp(cx)).astype(BF16)
    dec2 = jnp.exp(tot)

    q = q_ref[...].astype(F32)
    k = k_ref[...].astype(F32)
    v = v_ref[...]
    gf1, gf2 = _split2(gf_ref[...])
    gb1, gb2 = _split2(gb_ref[...])
    cf = _dot(tri_ref[0:C, :], jnp.concatenate([gf1, gf2], axis=0))
    cb = _dot(tri_ref[C:2 * C, :], jnp.concatenate([gb1, gb2], axis=0))
    g4 = jnp.concatenate([gf1, gb1, gf2, gb2], axis=0)

    levels = [C >> (i + 1) for i in range(C.bit_length() - 1)]
    operands = []
    for w in levels:
        if w in GLA_MXU_LEVELS:
            base = 2 * C * GLA_MXU_LEVELS.index(w)
            arg = _dot(tab_ref[base:base + 2 * C, :], g4)
            arg_q, arg_k = arg[:C], arg[C:]
        else:
            odd = (rowi & w) != 0
            d_f = cf - _pair_ref_rows(cf, w)
            d_b = cb - _pair_ref_rows(cb, w)
            arg_q = jnp.where(odd, d_f, -d_b)
            arg_k = jnp.where(odd, d_b, -d_f)
        operands.append(((q * jnp.exp(arg_q)).astype(BF16), (k * jnp.exp(arg_k)).astype(BF16)))
    operands.append((q_ref[...], k_ref[...]))
    cf_last = cf[C - 1:C, :]
    qe = (q * jnp.exp(cf)).astype(BF16)
    ke = (k * jnp.exp(cf_last - cf)).astype(BF16)
    dec = jnp.exp(cf_last)
    yield

    def head_blocks(qt, kt):
        out = []
        for h in range(GLA_HEADS):
            pr = slice((h // 2) * LANES, (h // 2 + 1) * LANES)
            keep = head_lo if h % 2 == 0 else jnp.logical_not(head_lo)
            out.append(_dot_nt(jnp.where(keep, qt[:, pr], jnp.zeros((), BF16)), kt[:, pr]))
        return out

    blocks = [head_blocks(qt, kt) for qt, kt in operands]
    yield

    a = [2.0 * blocks[-1][h] for h in range(GLA_HEADS)]
    for w, blk in zip(levels, blocks[:-1]):
        sel = hb == w
        a = [jnp.where(sel, blk[h], a[h]) for h in range(GLA_HEADS)]

    outs = []
    for p in range(2):
        pr = slice(p * LANES, (p + 1) * LANES)
        vr = slice(p * 2 * GLA_DV, (p + 1) * 2 * GLA_DV)
        st = sb_ref[p]
        outs.append(_dot_nt(qe2[:, pr], st.astype(BF16)))
        upd = _dot_tn(v2[:, vr], ke2[:, pr])
        sb_ref[p] = dec2[:, pr] * st + jnp.where(sdiag, upd, 0.0)
    ob_ref[...] = jnp.concatenate(outs, axis=1)

    outs = []
    for p in range(2):
        pr = slice(p * LANES, (p + 1) * LANES)
        vr = slice(p * 2 * GLA_DV, (p + 1) * 2 * GLA_DV)
        st = sf_ref[p]
        inter = _dot_nt(qe[:, pr], st.astype(BF16))
        for hh in range(2):
            h = 2 * p + hh
            vh = slice(h * GLA_DV, (h + 1) * GLA_DV)
            outs.append(_dot(a[h].astype(BF16), v[:, vh]) + inter[:, hh * GLA_DV:(hh + 1) * GLA_DV])
        upd = _dot_tn(v[:, vr], ke[:, pr])
        sf_ref[p] = dec[:, pr] * st + jnp.where(sdiag, upd, 0.0)
    om_ref[...] = jnp.concatenate(outs, axis=1)
    yield


def _gla_call(gq, gk, gv, gf, gb, B, S):
    M = gq.shape[0]
    rows = GLA_SUB * GLA_C
    n = S // rows
    fw = lambda width: pl.BlockSpec((rows, width), lambda b, i: (b * n + i, 0))
    bw = lambda width: pl.BlockSpec((rows, width), lambda b, i: (b * n + n - 1 - i, 0))
    tables = _gla_tables(GLA_C)
    return pl.pallas_call(
        _gla_kernel, grid=(B, n),
        in_specs=[pl.BlockSpec(t.shape, lambda b, i: (0, 0)) for t in tables]
        + [fw(GLA_QK), fw(GLA_QK), fw(GLA_WIDTH), fw(GLA_QK), fw(GLA_QK),
           bw(GLA_QK), bw(GLA_QK), bw(GLA_WIDTH), bw(GLA_QK)],
        out_specs=[fw(GLA_WIDTH), bw(GLA_WIDTH)],
        out_shape=[jax.ShapeDtypeStruct((M, GLA_WIDTH), F32)] * 2,
        scratch_shapes=[pltpu.VMEM((2, 2 * GLA_DV, 2 * GLA_DK), F32)] * 2,
        compiler_params=pltpu.CompilerParams(dimension_semantics=("parallel", "arbitrary"),
                                             vmem_limit_bytes=VMEM_LIMIT),
        name="gla")(*tables, gq, gk, gv, gf, gb, gq, gk, gv, gb)


def _layer_norm(h, g, b):
    mu = jnp.mean(h, axis=-1, keepdims=True)
    d = h - mu
    var = jnp.mean(d * d, axis=-1, keepdims=True)
    return d * lax.rsqrt(var + LN_EPS) * g + b


def _lockstep(parts):
    parts = list(parts)
    while parts:
        parts = [p for p in parts if next(p, StopIteration) is not StopIteration]


def _row_groups(ref, count):
    rows = ref.shape[0] // count
    return [ref.at[pl.ds(i * rows, rows)] for i in range(count)]


def _merge_rows(x_ref, ya_ref, om_ref, ob_ref, rs_ref, wg_ref, bg_ref, gn_ref,
                wa_ref, wb_ref, wo_ref, lg_ref, lb_ref, o_ref, alpha):
    x = x_ref[...]
    gate_logits = _dot(x.astype(BF16), wg_ref[...]) + bg_ref[...]
    pa = _dot(ya_ref[...], wa_ref[...])
    o = om_ref[...] + ob_ref[...]
    gn = gn_ref[...]
    cols = []
    for h in range(GLA_HEADS):
        oh = o[:, h * GLA_DV:(h + 1) * GLA_DV]
        ms = jnp.mean(oh * oh, axis=-1, keepdims=True)
        cols.append(oh * lax.rsqrt(ms + RMS_EPS) * gn)
    yb = (jnp.concatenate(cols, axis=1) * rs_ref[...].astype(F32)).astype(BF16)
    yield
    pb = _dot(yb, wb_ref[...])
    gates = jax.nn.sigmoid(gate_logits)
    mixin = (gates[:, :D_MODEL] * pa + gates[:, D_MODEL:] * pb).astype(BF16)
    yield
    mix = _dot(mixin, wo_ref[...])
    yield
    o_ref[...] = _layer_norm(alpha * x + mix, lg_ref[...], lb_ref[...])


def _merge_kernel(x_ref, ya_ref, om_ref, ob_ref, rs_ref, wg_ref, bg_ref, gn_ref,
                  wa_ref, wb_ref, wo_ref, lg_ref, lb_ref, o_ref, *, alpha):
    tiles = [_row_groups(r, ROW_GROUPS) for r in (x_ref, ya_ref, om_ref, ob_ref, rs_ref, o_ref)]
    _lockstep(_merge_rows(x, ya, om, ob, rs, wg_ref, bg_ref, gn_ref, wa_ref, wb_ref, wo_ref,
                          lg_ref, lb_ref, o, alpha) for x, ya, om, ob, rs, o in zip(*tiles))


def _merge_call(x2, ya, om, ob, rs, w, alpha):
    M = x2.shape[0]
    tm = MERGE_TM
    full = lambda a: pl.BlockSpec(a.shape, lambda i: (0,) * a.ndim)
    tok = lambda n: pl.BlockSpec((tm, n), lambda i: (i, 0))
    ws = [w["wgate"], w["bgate"], w["gnorm"], w["wbra"], w["wbrb"], w["wout"], w["ln1g"], w["ln1b"]]
    return pl.pallas_call(
        functools.partial(_merge_kernel, alpha=alpha), grid=(M // tm,),
        in_specs=[tok(D_MODEL), tok(512), tok(512), tok(512), tok(512)] + [full(a) for a in ws],
        out_specs=tok(D_MODEL), out_shape=jax.ShapeDtypeStruct((M, D_MODEL), F32),
        compiler_params=pltpu.CompilerParams(dimension_semantics=("parallel",),
                                             vmem_limit_bytes=VMEM_LIMIT),
        name="merge")(x2, ya, om, ob, rs, *ws)


def _mlp_kernel(x_ref, w1_ref, w2_ref, lg_ref, lb_ref, o_ref, *, alpha):
    x = x_ref[...]
    h = jnp.maximum(_dot(x.astype(BF16), w1_ref[...]), 0.0)
    h = _dot((h * h).astype(BF16), w2_ref[...])
    o_ref[...] = _layer_norm(alpha * x + h, lg_ref[...], lb_ref[...])


def _mlp_call(x1, w, alpha):
    M = x1.shape[0]
    tm = MLP_TM
    once = lambda a: pl.BlockSpec(a.shape, lambda i: (0,) * a.ndim, pipeline_mode=pl.Buffered(1))
    tok = pl.BlockSpec((tm, D_MODEL), lambda i: (i, 0))
    ws = [w["w1"], w["w2"], w["ln2g"], w["ln2b"]]
    return pl.pallas_call(
        functools.partial(_mlp_kernel, alpha=alpha), grid=(M // tm,),
        in_specs=[tok] + [once(a) for a in ws],
        out_specs=tok, out_shape=jax.ShapeDtypeStruct((M, D_MODEL), F32),
        compiler_params=pltpu.CompilerParams(dimension_semantics=("parallel",),
                                             vmem_limit_bytes=VMEM_LIMIT),
        name="mlp")(x1, *ws)


def _rope_tables(S):
    half = ROT_DIM // 2
    inv = 1.0 / (ROPE_THETA ** (jnp.arange(0, ROT_DIM, 2, dtype=F32) / ROT_DIM))
    ang = jnp.arange(S, dtype=F32)[:, None] * inv[None, :]
    cos, sin = jnp.cos(ang), jnp.sin(ang)
    d = jnp.arange(LANES) % DA_DK
    idx = d % half
    cos_t = jnp.where(d[None, :] < ROT_DIM, cos[:, idx], 1.0)
    sin_a = jnp.where(d[None, :] < half, -sin[:, idx], 0.0)
    sin_b = jnp.where((d[None, :] >= half) & (d[None, :] < ROT_DIM), sin[:, idx], 0.0)
    return cos_t.astype(F32), sin_a.astype(F32), sin_b.astype(F32)


def _layer_weights(l, w_in, subln_g, w_dec_f, b_dec_f, w_dec_b, b_dec_b, gla_norm_g, w_br_a, w_br_b,
                   b_gate, w_out, ln1_g, ln1_b, w_mlp1, w_mlp2, ln2_g, ln2_b):
    offs = [0]
    for s in IN_SIZES:
        offs.append(offs[-1] + s)
    col = lambda i: w_in[l][:, offs[i]:offs[i + 1]]
    zeros = jnp.zeros((GLA_RANK, GLA_QK), F32)
    row = lambda a: a[l].reshape(1, -1).astype(F32)
    return {
        "wq": (col(0) * (DA_DK ** -0.5 * LOG2E)).astype(BF16), "wk": col(1).astype(BF16), "wv": col(2).astype(BF16),
        "wgq": (col(3) * GLA_DK ** -0.5).astype(BF16), "wgk": col(4).astype(BF16),
        "wgv": col(5).astype(BF16), "wgr": col(6).astype(BF16),
        "wz": jnp.concatenate([col(7), col(8)], axis=1).astype(BF16),
        "wdec": jnp.concatenate([jnp.concatenate([w_dec_f[l], zeros], axis=1),
                                 jnp.concatenate([zeros, w_dec_b[l]], axis=1)], axis=0).astype(BF16),
        "bdec": jnp.concatenate([b_dec_f[l], b_dec_b[l]]).reshape(1, -1).astype(F32),
        "wgate": col(9).astype(BF16), "bgate": row(b_gate),
        "subg": subln_g[l].reshape(-1, 1).astype(F32), "gnorm": row(gla_norm_g),
        "wbra": w_br_a[l].astype(BF16), "wbrb": w_br_b[l].astype(BF16), "wout": w_out[l].astype(BF16),
        "ln1g": row(ln1_g), "ln1b": row(ln1_b),
        "w1": w_mlp1[l].astype(BF16), "w2": w_mlp2[l].astype(BF16),
        "ln2g": row(ln2_g), "ln2b": row(ln2_b),
    }


def kernel(x_prompt, x_sample, w_in, lam_q1, lam_k1, lam_q2, lam_k2, subln_g, w_dec_f, b_dec_f, w_dec_b, b_dec_b, gla_norm_g, w_br_a, w_br_b, b_gate, w_out, ln1_g, ln1_b, w_mlp1, w_mlp2, ln2_g, ln2_b):
    depth = w_in.shape[0]
    alpha = (2 * depth) ** 0.25
    layers = []
    for l in range(depth):
        w = _layer_weights(l, w_in, subln_g, w_dec_f, b_dec_f, w_dec_b, b_dec_b, gla_norm_g,
                           w_br_a, w_br_b, b_gate, w_out, ln1_g, ln1_b, w_mlp1, w_mlp2, ln2_g, ln2_b)
        lamv = jnp.stack([lam_q1[l], lam_k1[l], lam_q2[l], lam_k2[l]]).astype(F32)
        layers.append((w, lamv, 0.8 - 0.6 * math.exp(-0.3 * l)))

    def trunk(x):
        B, S, D = x.shape
        tabs = _rope_tables(S)
        x2 = x.reshape(B * S, D)
        for w, lamv, lam_init in layers:
            q, k, vt, gq, gk, gv, rs, gf, gb = _proj_call(x2, B, S, w, tabs)
            ya = _attn_call(q.reshape(B, S, 512), k.reshape(B, S, 512), vt, lamv, w["subg"], lam_init)
            om, ob = _gla_call(gq, gk, gv, gf, gb, B, S)
            x1 = _merge_call(x2, ya.reshape(B * S, 512), om, ob, rs, w, alpha)
            x2 = _mlp_call(x1, w, alpha)
        return x2.reshape(B, S, D)

    return (trunk(x_prompt), trunk(x_sample))
```

```python
import functools
import math

import jax
import jax.numpy as jnp
import numpy as np
from jax import lax
from jax.experimental import pallas as pl
from jax.experimental.pallas import tpu as pltpu

F32 = jnp.float32
BF16 = jnp.bfloat16

D_MODEL = 1024
DA_HEADS = 4
DA_DK = 64
DA_DV = 128
DA_WIDTH = 512
ROT_DIM = 16
ROPE_THETA = 500000.0
GLA_HEADS = 4
GLA_DK = 64
GLA_DV = 128
GLA_QK = 256
GLA_WIDTH = 512
GLA_RANK = 16
GLA_TAU = 16.0
D_FF = 4096
LN_EPS = 1e-5
RMS_EPS = 1e-6
IN_SIZES = (512, 512, 512, 256, 256, 512, 512, 16, 16, 2048)

LANES = 128
VMEM_LIMIT = 56 * 1024 * 1024

PROJ_TM = 512
ATT_TQ = 128
ATT_TK = 1024
ATT_RC = 128
LOG2E = math.log2(math.e)
GLA_C = 128
GLA_SUB = 4
MERGE_TM = 512
MLP_TM = 512
ROW_GROUPS = 2


def _dot(a, b):
    return jnp.dot(a, b, preferred_element_type=F32)


def _dot_nt(a, b):
    return lax.dot_general(a, b, (((1,), (1,)), ((), ())), preferred_element_type=F32)


def _dot_tn(a, b):
    return lax.dot_general(a, b, (((0,), (0,)), ((), ())), preferred_element_type=F32)


def _proj_kernel(x_ref, wq_ref, wk_ref, wv_ref, wgq_ref, wgk_ref, wgv_ref, wgr_ref, wz_ref,
                 wdec_ref, bdec_ref, cos_ref, sa_ref, sb_ref,
                 q_ref, k_ref, vt_ref, gq_ref, gk_ref, gv_ref, rs_ref, gf_ref, gb_ref):
    x = x_ref[...].astype(BF16)
    cos = cos_ref[...]
    sin_a = sa_ref[...]
    sin_b = sb_ref[...]

    def rope(t):
        cols = []
        for j in range(t.shape[1] // LANES):
            blk = t[:, j * LANES:(j + 1) * LANES]
            cols.append(blk * cos + pltpu.roll(blk, LANES - ROT_DIM // 2, 1) * sin_a
                        + pltpu.roll(blk, ROT_DIM // 2, 1) * sin_b)
        return jnp.concatenate(cols, axis=1)

    z = _dot(x, wz_ref[...])
    zz = _dot(z.astype(BF16), wdec_ref[...]) + bdec_ref[...]
    g = -(jnp.maximum(-zz, 0.0) + jnp.log1p(jnp.exp(-jnp.abs(zz)))) / GLA_TAU
    gf_ref[...] = g[:, :GLA_QK]
    gb_ref[...] = g[:, GLA_QK:]
    r = _dot(x, wgr_ref[...])
    rs_ref[...] = (r * jax.nn.sigmoid(r)).astype(BF16)
    q_ref[...] = rope(_dot(x, wq_ref[...])).astype(BF16)
    k_ref[...] = rope(_dot(x, wk_ref[...])).astype(BF16)
    vt_ref[...] = _dot(x, wv_ref[...]).T.astype(BF16)
    gq_ref[...] = _dot(x, wgq_ref[...]).astype(BF16)
    gk_ref[...] = _dot(x, wgk_ref[...]).astype(BF16)
    gv_ref[...] = _dot(x, wgv_ref[...]).astype(BF16)


def _proj_call(x2, B, S, w, tabs):
    M = x2.shape[0]
    tm = PROJ_TM
    ns = S // tm
    full = lambda a: pl.BlockSpec(a.shape, lambda i: (0,) * a.ndim)
    tok = lambda n: pl.BlockSpec((tm, n), lambda i: (i, 0))
    tab = pl.BlockSpec((tm, LANES), lambda i: (i % ns, 0))
    ins = [x2, w["wq"], w["wk"], w["wv"], w["wgq"], w["wgk"], w["wgv"], w["wgr"], w["wz"],
           w["wdec"], w["bdec"], tabs[0], tabs[1], tabs[2]]
    in_specs = [tok(D_MODEL)] + [full(a) for a in ins[1:11]] + [tab, tab, tab]
    out_shape = [
        jax.ShapeDtypeStruct((M, 512), BF16),
        jax.ShapeDtypeStruct((M, 512), BF16),
        jax.ShapeDtypeStruct((B, 512, S), BF16),
        jax.ShapeDtypeStruct((M, GLA_QK), BF16),
        jax.ShapeDtypeStruct((M, GLA_QK), BF16),
        jax.ShapeDtypeStruct((M, GLA_WIDTH), BF16),
        jax.ShapeDtypeStruct((M, GLA_WIDTH), BF16),
        jax.ShapeDtypeStruct((M, GLA_QK), F32),
        jax.ShapeDtypeStruct((M, GLA_QK), F32),
    ]
    out_specs = [tok(512), tok(512),
                 pl.BlockSpec((None, 512, tm), lambda i: (i // ns, 0, i % ns)),
                 tok(GLA_QK), tok(GLA_QK), tok(GLA_WIDTH), tok(GLA_WIDTH), tok(GLA_QK), tok(GLA_QK)]
    return pl.pallas_call(
        _proj_kernel, grid=(M // tm,), in_specs=in_specs, out_specs=out_specs, out_shape=out_shape,
        compiler_params=pltpu.CompilerParams(dimension_semantics=("parallel",),
                                             vmem_limit_bytes=VMEM_LIMIT),
        name="proj")(*ins)


def _attn_kernel(lamv_ref, g_ref, q_ref, k_ref, vt_ref, o_ref, s0_ref, s1_ref, p0_ref, p1_ref,
                 *, seq, lam_init):
    tq, tk, rc = ATT_TQ, ATT_TK, ATT_RC
    ncol = 2 * tq
    nch = tk // rc
    n = seq // tk
    nq = seq // tq
    s_bufs = (s0_ref, s1_ref)
    p_bufs = (p0_ref, p1_ref)

    lv = lamv_ref[...]
    lam = (jnp.exp(jnp.sum(lv[0:1] * lv[1:2], keepdims=True))
           - jnp.exp(jnp.sum(lv[2:3] * lv[3:4], keepdims=True)) + lam_init)
    gain = g_ref[...] * (1.0 - lam_init)

    def make_qq(qi):
        qt = q_ref[pl.ds(pl.multiple_of(qi * tq, tq), tq), :].astype(F32).T
        row = lax.broadcasted_iota(jnp.int32, qt.shape, 0)
        return jnp.concatenate([jnp.where(row < DA_DK, qt, 0.0),
                                jnp.where(row >= DA_DK, qt, 0.0)], axis=1).astype(BF16)

    def qk_stage(qq, t):
        sc = _dot(k_ref[t * tk:(t + 1) * tk, :], qq)
        s_bufs[t % 2][...] = sc
        return jnp.max(sc.reshape(tk // 8, 8, ncol), axis=0)

    def softmax_stage(t, mx, m, l):
        m_new = jnp.maximum(m, jnp.max(mx, axis=0, keepdims=True))
        alpha = jnp.exp2(m - m_new)
        mb = jnp.broadcast_to(m_new, (8, ncol))[None]
        ls = None
        for c in range(nch):
            rows = slice(c * rc, (c + 1) * rc)
            p = jnp.exp2(s_bufs[t % 2][rows, :].reshape(rc // 8, 8, ncol) - mb)
            part = jnp.sum(p, axis=0)
            ls = part if ls is None else ls + part
            p_bufs[t % 2][rows, :] = p.reshape(rc, ncol).astype(BF16)
        return m_new, alpha * l + jnp.sum(ls, axis=0, keepdims=True), alpha

    def pv_stage(t, alpha, acc):
        return alpha * acc + _dot(vt_ref[:, t * tk:(t + 1) * tk], p_bufs[t % 2][...])

    def finalize(qi, l, acc):
        on = acc / l
        o = on[:, :tq] - lam * on[:, tq:]
        ms = jnp.mean(o * o, axis=0, keepdims=True)
        y = o * lax.rsqrt(ms + RMS_EPS) * gain
        o_ref[pl.ds(pl.multiple_of(qi * tq, tq), tq), :] = y.T.astype(BF16)

    m_init = jnp.full((1, ncol), -jnp.inf, F32)
    l_init = jnp.zeros((1, ncol), F32)
    acc_init = jnp.zeros((DA_DV, ncol), F32)

    def query_tile(qi, qq, state, first):
        mx, m, l, alpha, acc = state
        for t in range(n):
            mx_new = qk_stage(qq, t)
            if t == 0 and not first:
                acc = pv_stage(n - 2, alpha, acc)
                m, l, alpha = softmax_stage(n - 1, mx, m, l)
            elif t == 1:
                if not first:
                    acc = pv_stage(n - 1, alpha, acc)
                    finalize(qi - 1, l, acc)
                m, l, alpha = softmax_stage(0, mx, m_init, l_init)
                acc = acc_init
            elif t >= 2:
                acc = pv_stage(t - 2, alpha, acc)
                m, l, alpha = softmax_stage(t - 1, mx, m, l)
            mx = mx_new
        return mx, m, l, alpha, acc

    state = (jnp.zeros((8, ncol), F32), m_init, l_init, jnp.ones((1, ncol), F32), acc_init)
    state = query_tile(0, make_qq(0), state, True)

    def body(qi, carry):
        qq, state = carry
        qq_next = make_qq(jnp.minimum(qi + 1, nq - 1))
        return qq_next, query_tile(qi, qq, state, False)

    _, state = lax.fori_loop(1, nq, body, (make_qq(1), state))
    mx, m, l, alpha, acc = state
    acc = pv_stage(n - 2, alpha, acc)
    m, l, alpha = softmax_stage(n - 1, mx, m, l)
    acc = pv_stage(n - 1, alpha, acc)
    finalize(nq - 1, l, acc)


def _attn_call(q, k, vt, lamv, subg, lam_init):
    B, S, _ = q.shape
    seq_blk = pl.BlockSpec((None, S, LANES), lambda b, h: (b, 0, h))
    return pl.pallas_call(
        functools.partial(_attn_kernel, seq=S, lam_init=lam_init),
        grid=(B, DA_HEADS),
        in_specs=[
            pl.BlockSpec(lamv.shape, lambda b, h: (0, 0)),
            pl.BlockSpec(subg.shape, lambda b, h: (0, 0)),
            seq_blk, seq_blk,
            pl.BlockSpec((None, LANES, S), lambda b, h: (b, h, 0)),
        ],
        out_specs=seq_blk,
        out_shape=jax.ShapeDtypeStruct((B, S, DA_WIDTH), BF16),
        scratch_shapes=[pltpu.VMEM((ATT_TK, 2 * ATT_TQ), F32)] * 2
        + [pltpu.VMEM((ATT_TK, 2 * ATT_TQ), BF16)] * 2,
        compiler_params=pltpu.CompilerParams(
            dimension_semantics=("parallel", "parallel"),
            vmem_limit_bytes=VMEM_LIMIT),
        name="diff_attn")(lamv, subg, q, k, vt)


GLA_MXU_LEVELS = (4, 2, 1)


def _split2(g):
    g1 = g.astype(BF16)
    return g1, (g - g1.astype(F32)).astype(BF16)


def _gla_tables(C):
    t = np.arange(C)
    tri_i = (t[None, :] <= t[:, None]).astype(np.float32)
    tri_s = (t[None, :] < t[:, None]).astype(np.float32)
    blocks = []
    for w in GLA_MXU_LEVELS:
        rho = (t // (2 * w)) * 2 * w + w
        odd = ((t & w) != 0).astype(np.float32)[:, None]
        d_f = tri_i - tri_i[rho]
        d_b = tri_s - tri_s[rho]
        blocks.append(np.concatenate([odd * d_f, (odd - 1.0) * d_b], 1))
        blocks.append(np.concatenate([(odd - 1.0) * d_f, odd * d_b], 1))
    tab = np.concatenate(blocks, 0)
    x = t[:, None] ^ t[None, :]
    hb = np.where(x > 0, 2 ** np.floor(np.log2(np.maximum(x, 1))), 0).astype(np.int32)
    tri = np.concatenate([tri_i, tri_s], 0)
    return (jnp.asarray(np.concatenate([tab, tab], 1), BF16),
            jnp.asarray(np.concatenate([tri, tri], 1), BF16), jnp.asarray(hb))


def _pair_ref_rows(c, w):
    C, n = c.shape
    c3 = c.reshape(C // (2 * w), 2 * w, n)
    return jnp.broadcast_to(c3[:, w:w + 1, :], c3.shape).reshape(C, n)


def _gla_kernel(tab_ref, tri_ref, hb_ref, q_ref, k_ref, v_ref, gf_ref, gb_ref,
                qb_ref, kb_ref, vb_ref, gbb_ref, om_ref, ob_ref, sf_ref, sb_ref):
    @pl.when(pl.program_id(1) == 0)
    def _():
        sf_ref[...] = jnp.zeros_like(sf_ref)
        sb_ref[...] = jnp.zeros_like(sb_ref)

    chunks = []
    for j in range(GLA_SUB):
        rf = pl.ds(j * GLA_C, GLA_C)
        rb = pl.ds((GLA_SUB - 1 - j) * GLA_C, GLA_C)
        chunks.append(_gla_chunk(tab_ref, tri_ref, hb_ref, q_ref.at[rf], k_ref.at[rf], v_ref.at[rf],
                                 gf_ref.at[rf], gb_ref.at[rf], qb_ref.at[rb], kb_ref.at[rb],
                                 vb_ref.at[rb], gbb_ref.at[rb], om_ref.at[rf], ob_ref.at[rb],
                                 sf_ref, sb_ref))
    for _ in range(GLA_PHASES):
        for chunk in chunks:
            next(chunk)


GLA_PHASES = 3


def _gla_chunk(tab_ref, tri_ref, hb_ref, q_ref, k_ref, v_ref, gf_ref, gb_ref,
               qb_ref, kb_ref, vb_ref, gbb_ref, om_ref, ob_ref, sf_ref, sb_ref):
    C = GLA_C
    hb = hb_ref[...]
    rowi = lax.broadcasted_iota(jnp.int32, (C, 1), 0)
    lane = lax.broadcasted_iota(jnp.int32, (C, LANES), 1)
    head_lo = lane < GLA_DK
    sr = lax.broadcasted_iota(jnp.int32, (2 * GLA_DV, 2 * GLA_DK), 0)
    sc = lax.broadcasted_iota(jnp.int32, (2 * GLA_DV, 2 * GLA_DK), 1)
    sdiag = (sr < GLA_DV) == (sc < GLA_DK)

    q2 = qb_ref[...].astype(F32)
    k2 = kb_ref[...].astype(F32)
    v2 = vb_ref[...]
    g2 = gbb_ref[...]
    cx = _dot(tri_ref[C:2 * C, :], jnp.concatenate(_split2(g2), axis=0))
    tot = cx[C - 1:C, :] + g2[C - 1:C, :]
    qe2 = (q2 * jnp.exp(tot - cx)).astype(BF16)
    ke2 = (k2 * jnp.exp(cx)).astype(BF16)
    dec2 = jnp.exp(tot)

    q = q_ref[...].astype(F32)
    k = k_ref[...].astype(F32)
    v = v_ref[...]
    gf1, gf2 = _split2(gf_ref[...])
    gb1, gb2 = _split2(gb_ref[...])
    cf = _dot(tri_ref[0:C, :], jnp.concatenate([gf1, gf2], axis=0))
    cb = _dot(tri_ref[C:2 * C, :], jnp.concatenate([gb1, gb2], axis=0))
    g4 = jnp.concatenate([gf1, gb1, gf2, gb2], axis=0)

    levels = [C >> (i + 1) for i in range(C.bit_length() - 1)]
    operands = []
    for w in levels:
        if w in GLA_MXU_LEVELS:
            base = 2 * C * GLA_MXU_LEVELS.index(w)
            arg = _dot(tab_ref[base:base + 2 * C, :], g4)
            arg_q, arg_k = arg[:C], arg[C:]
        else:
            odd = (rowi & w) != 0
            d_f = cf - _pair_ref_rows(cf, w)
            d_b = cb - _pair_ref_rows(cb, w)
            arg_q = jnp.where(odd, d_f, -d_b)
            arg_k = jnp.where(odd, d_b, -d_f)
        operands.append(((q * jnp.exp(arg_q)).astype(BF16), (k * jnp.exp(arg_k)).astype(BF16)))
    operands.append((q_ref[...], k_ref[...]))
    cf_last = cf[C - 1:C, :]
    qe = (q * jnp.exp(cf)).astype(BF16)
    ke = (k * jnp.exp(cf_last - cf)).astype(BF16)
    dec = jnp.exp(cf_last)
    yield

    def head_blocks(qt, kt):
        out = []
        for h in range(GLA_HEADS):
            pr = slice((h // 2) * LANES, (h // 2 + 1) * LANES)
            keep = head_lo if h % 2 == 0 else jnp.logical_not(head_lo)
            out.append(_dot_nt(jnp.where(keep, qt[:, pr], jnp.zeros((), BF16)), kt[:, pr]))
        return out

    blocks = [head_blocks(qt, kt) for qt, kt in operands]
    yield

    a = [2.0 * blocks[-1][h] for h in range(GLA_HEADS)]
    for w, blk in zip(levels, blocks[:-1]):
        sel = hb == w
        a = [jnp.where(sel, blk[h], a[h]) for h in range(GLA_HEADS)]

    outs = []
    for p in range(2):
        pr = slice(p * LANES, (p + 1) * LANES)
        vr = slice(p * 2 * GLA_DV, (p + 1) * 2 * GLA_DV)
        st = sb_ref[p]
        outs.append(_dot_nt(qe2[:, pr], st.astype(BF16)))
        upd = _dot_tn(v2[:, vr], ke2[:, pr])
        sb_ref[p] = dec2[:, pr] * st + jnp.where(sdiag, upd, 0.0)
    ob_ref[...] = jnp.concatenate(outs, axis=1)

    outs = []
    for p in range(2):
        pr = slice(p * LANES, (p + 1) * LANES)
        vr = slice(p * 2 * GLA_DV, (p + 1) * 2 * GLA_DV)
        st = sf_ref[p]
        inter = _dot_nt(qe[:, pr], st.astype(BF16))
        for hh in range(2):
            h = 2 * p + hh
            vh = slice(h * GLA_DV, (h + 1) * GLA_DV)
            outs.append(_dot(a[h].astype(BF16), v[:, vh]) + inter[:, hh * GLA_DV:(hh + 1) * GLA_DV])
        upd = _dot_tn(v[:, vr], ke[:, pr])
        sf_ref[p] = dec[:, pr] * st + jnp.where(sdiag, upd, 0.0)
    om_ref[...] = jnp.concatenate(outs, axis=1)
    yield


def _gla_call(gq, gk, gv, gf, gb, B, S):
    M = gq.shape[0]
    rows = GLA_SUB * GLA_C
    n = S // rows
    fw = lambda width: pl.BlockSpec((rows, width), lambda b, i: (b * n + i, 0))
    bw = lambda width: pl.BlockSpec((rows, width), lambda b, i: (b * n + n - 1 - i, 0))
    tables = _gla_tables(GLA_C)
    return pl.pallas_call(
        _gla_kernel, grid=(B, n),
        in_specs=[pl.BlockSpec(t.shape, lambda b, i: (0, 0)) for t in tables]
        + [fw(GLA_QK), fw(GLA_QK), fw(GLA_WIDTH), fw(GLA_QK), fw(GLA_QK),
           bw(GLA_QK), bw(GLA_QK), bw(GLA_WIDTH), bw(GLA_QK)],
        out_specs=[fw(GLA_WIDTH), bw(GLA_WIDTH)],
        out_shape=[jax.ShapeDtypeStruct((M, GLA_WIDTH), F32)] * 2,
        scratch_shapes=[pltpu.VMEM((2, 2 * GLA_DV, 2 * GLA_DK), F32)] * 2,
        compiler_params=pltpu.CompilerParams(dimension_semantics=("parallel", "arbitrary"),
                                             vmem_limit_bytes=VMEM_LIMIT),
        name="gla")(*tables, gq, gk, gv, gf, gb, gq, gk, gv, gb)


def _layer_norm(h, g, b):
    mu = jnp.mean(h, axis=-1, keepdims=True)
    d = h - mu
    var = jnp.mean(d * d, axis=-1, keepdims=True)
    return d * lax.rsqrt(var + LN_EPS) * g + b


def _lockstep(parts):
    parts = list(parts)
    while parts:
        parts = [p for p in parts if next(p, StopIteration) is not StopIteration]


def _row_groups(ref, count):
    rows = ref.shape[0] // count
    return [ref.at[pl.ds(i * rows, rows)] for i in range(count)]


def _merge_rows(x_ref, ya_ref, om_ref, ob_ref, rs_ref, wg_ref, bg_ref, gn_ref,
                wa_ref, wb_ref, wo_ref, lg_ref, lb_ref, o_ref, alpha):
    x = x_ref[...]
    gate_logits = _dot(x.astype(BF16), wg_ref[...]) + bg_ref[...]
    pa = _dot(ya_ref[...], wa_ref[...])
    o = om_ref[...] + ob_ref[...]
    gn = gn_ref[...]
    cols = []
    for h in range(GLA_HEADS):
        oh = o[:, h * GLA_DV:(h + 1) * GLA_DV]
        ms = jnp.mean(oh * oh, axis=-1, keepdims=True)
        cols.append(oh * lax.rsqrt(ms + RMS_EPS) * gn)
    yb = (jnp.concatenate(cols, axis=1) * rs_ref[...].astype(F32)).astype(BF16)
    yield
    pb = _dot(yb, wb_ref[...])
    gates = jax.nn.sigmoid(gate_logits)
    mixin = (gates[:, :D_MODEL] * pa + gates[:, D_MODEL:] * pb).astype(BF16)
    yield
    mix = _dot(mixin, wo_ref[...])
    yield
    o_ref[...] = _layer_norm(alpha * x + mix, lg_ref[...], lb_ref[...])


def _merge_kernel(x_ref, ya_ref, om_ref, ob_ref, rs_ref, wg_ref, bg_ref, gn_ref,
                  wa_ref, wb_ref, wo_ref, lg_ref, lb_ref, o_ref, *, alpha):
    tiles = [_row_groups(r, ROW_GROUPS) for r in (x_ref, ya_ref, om_ref, ob_ref, rs_ref, o_ref)]
    _lockstep(_merge_rows(x, ya, om, ob, rs, wg_ref, bg_ref, gn_ref, wa_ref, wb_ref, wo_ref,
                          lg_ref, lb_ref, o, alpha) for x, ya, om, ob, rs, o in zip(*tiles))


def _merge_call(x2, ya, om, ob, rs, w, alpha):
    M = x2.shape[0]
    tm = MERGE_TM
    full = lambda a: pl.BlockSpec(a.shape, lambda i: (0,) * a.ndim)
    tok = lambda n: pl.BlockSpec((tm, n), lambda i: (i, 0))
    ws = [w["wgate"], w["bgate"], w["gnorm"], w["wbra"], w["wbrb"], w["wout"], w["ln1g"], w["ln1b"]]
    return pl.pallas_call(
        functools.partial(_merge_kernel, alpha=alpha), grid=(M // tm,),
        in_specs=[tok(D_MODEL), tok(512), tok(512), tok(512), tok(512)] + [full(a) for a in ws],
        out_specs=tok(D_MODEL), out_shape=jax.ShapeDtypeStruct((M, D_MODEL), F32),
        compiler_params=pltpu.CompilerParams(dimension_semantics=("parallel",),
                                             vmem_limit_bytes=VMEM_LIMIT),
        name="merge")(x2, ya, om, ob, rs, *ws)


def _mlp_rows(x_ref, w1_ref, w2_ref, lg_ref, lb_ref, o_ref, alpha):
    x = x_ref[...]
    h = jnp.maximum(_dot(x.astype(BF16), w1_ref[...]), 0.0)
    yield
    h = (h * h).astype(BF16)
    yield
    h = _dot(h, w2_ref[...])
    yield
    o_ref[...] = _layer_norm(alpha * x + h, lg_ref[...], lb_ref[...])


def _mlp_kernel(x_ref, w1_ref, w2_ref, lg_ref, lb_ref, o_ref, *, alpha):
    _lockstep(_mlp_rows(x, w1_ref, w2_ref, lg_ref, lb_ref, o, alpha)
              for x, o in zip(_row_groups(x_ref, ROW_GROUPS), _row_groups(o_ref, ROW_GROUPS)))


def _mlp_call(x1, w, alpha):
    M = x1.shape[0]
    tm = MLP_TM
    once = lambda a: pl.BlockSpec(a.shape, lambda i: (0,) * a.ndim, pipeline_mode=pl.Buffered(1))
    tok = pl.BlockSpec((tm, D_MODEL), lambda i: (i, 0))
    ws = [w["w1"], w["w2"], w["ln2g"], w["ln2b"]]
    return pl.pallas_call(
        functools.partial(_mlp_kernel, alpha=alpha), grid=(M // tm,),
        in_specs=[tok] + [once(a) for a in ws],
        out_specs=tok, out_shape=jax.ShapeDtypeStruct((M, D_MODEL), F32),
        compiler_params=pltpu.CompilerParams(dimension_semantics=("parallel",),
                                             vmem_limit_bytes=VMEM_LIMIT),
        name="mlp")(x1, *ws)


def _rope_tables(S):
    half = ROT_DIM // 2
    inv = 1.0 / (ROPE_THETA ** (jnp.arange(0, ROT_DIM, 2, dtype=F32) / ROT_DIM))
    ang = jnp.arange(S, dtype=F32)[:, None] * inv[None, :]
    cos, sin = jnp.cos(ang), jnp.sin(ang)
    d = jnp.arange(LANES) % DA_DK
    idx = d % half
    cos_t = jnp.where(d[None, :] < ROT_DIM, cos[:, idx], 1.0)
    sin_a = jnp.where(d[None, :] < half, -sin[:, idx], 0.0)
    sin_b = jnp.where((d[None, :] >= half) & (d[None, :] < ROT_DIM), sin[:, idx], 0.0)
    return cos_t.astype(F32), sin_a.astype(F32), sin_b.astype(F32)


def _layer_weights(l, w_in, subln_g, w_dec_f, b_dec_f, w_dec_b, b_dec_b, gla_norm_g, w_br_a, w_br_b,
                   b_gate, w_out, ln1_g, ln1_b, w_mlp1, w_mlp2, ln2_g, ln2_b):
    offs = [0]
    for s in IN_SIZES:
        offs.append(offs[-1] + s)
    col = lambda i: w_in[l][:, offs[i]:offs[i + 1]]
    zeros = jnp.zeros((GLA_RANK, GLA_QK), F32)
    row = lambda a: a[l].reshape(1, -1).astype(F32)
    return {
        "wq": (col(0) * (DA_DK ** -0.5 * LOG2E)).astype(BF16), "wk": col(1).astype(BF16), "wv": col(2).astype(BF16),
        "wgq": (col(3) * GLA_DK ** -0.5).astype(BF16), "wgk": col(4).astype(BF16),
        "wgv": col(5).astype(BF16), "wgr": col(6).astype(BF16),
        "wz": jnp.concatenate([col(7), col(8)], axis=1).astype(BF16),
        "wdec": jnp.concatenate([jnp.concatenate([w_dec_f[l], zeros], axis=1),
                                 jnp.concatenate([zeros, w_dec_b[l]], axis=1)], axis=0).astype(BF16),
        "bdec": jnp.concatenate([b_dec_f[l], b_dec_b[l]]).reshape(1, -1).astype(F32),
        "wgate": col(9).astype(BF16), "bgate": row(b_gate),
        "subg": subln_g[l].reshape(-1, 1).astype(F32), "gnorm": row(gla_norm_g),
        "wbra": w_br_a[l].astype(BF16), "wbrb": w_br_b[l].astype(BF16), "wout": w_out[l].astype(BF16),
        "ln1g": row(ln1_g), "ln1b": row(ln1_b),
        "w1": w_mlp1[l].astype(BF16), "w2": w_mlp2[l].astype(BF16),
        "ln2g": row(ln2_g), "ln2b": row(ln2_b),
    }


def kernel(x_prompt, x_sample, w_in, lam_q1, lam_k1, lam_q2, lam_k2, subln_g, w_dec_f, b_dec_f, w_dec_b, b_dec_b, gla_norm_g, w_br_a, w_br_b, b_gate, w_out, ln1_g, ln1_b, w_mlp1, w_mlp2, ln2_g, ln2_b):
    depth = w_in.shape[0]
    alpha = (2 * depth) ** 0.25
    layers = []
    for l in range(depth):
        w = _layer_weights(l, w_in, subln_g, w_dec_f, b_dec_f, w_dec_b, b_dec_b, gla_norm_g,
                           w_br_a, w_br_b, b_gate, w_out, ln1_g, ln1_b, w_mlp1, w_mlp2, ln2_g, ln2_b)
        lamv = jnp.stack([lam_q1[l], lam_k1[l], lam_q2[l], lam_k2[l]]).astype(F32)
        layers.append((w, lamv, 0.8 - 0.6 * math.exp(-0.3 * l)))

    def trunk(x):
        B, S, D = x.shape
        assert D == D_MODEL and S % (2 * ATT_TK) == 0 and S % ATT_TQ == 0, (S, D)
        assert S % PROJ_TM == 0 and S % (GLA_SUB * GLA_C) == 0, S
        assert (B * S) % MERGE_TM == 0 and (B * S) % MLP_TM == 0, (B, S)
        tabs = _rope_tables(S)
        x2 = x.reshape(B * S, D)
        for w, lamv, lam_init in layers:
            q, k, vt, gq, gk, gv, rs, gf, gb = _proj_call(x2, B, S, w, tabs)
            ya = _attn_call(q.reshape(B, S, 512), k.reshape(B, S, 512), vt, lamv, w["subg"], lam_init)
            om, ob = _gla_call(gq, gk, gv, gf, gb, B, S)
            x1 = _merge_call(x2, ya.reshape(B * S, 512), om, ob, rs, w, alpha)
            x2 = _mlp_call(x1, w, alpha)
        return x2.reshape(B, S, D)

    return (trunk(x_prompt), trunk(x_sample))
```

```python
import functools
import math

import jax
import jax.numpy as jnp
import numpy as np
from jax import lax
from jax.experimental import pallas as pl
from jax.experimental.pallas import tpu as pltpu

F32 = jnp.float32
BF16 = jnp.bfloat16

D_MODEL = 1024
DA_HEADS = 4
DA_DK = 64
DA_DV = 128
DA_WIDTH = 512
ROT_DIM = 16
ROPE_THETA = 500000.0
GLA_HEADS = 4
GLA_DK = 64
GLA_DV = 128
GLA_QK = 256
GLA_WIDTH = 512
GLA_RANK = 16
GLA_TAU = 16.0
D_FF = 4096
LN_EPS = 1e-5
RMS_EPS = 1e-6
IN_SIZES = (512, 512, 512, 256, 256, 512, 512, 16, 16, 2048)

LANES = 128
VMEM_LIMIT = 56 * 1024 * 1024

PROJ_TM = 512
ATT_TQ = 128
ATT_TK = 1024
ATT_RC = 128
ATT_NBUF = 4
LOG2E = math.log2(math.e)
GLA_C = 128
GLA_SUB = 4
MERGE_TM = 512
MLP_TM = 512
ROW_GROUPS = 2


def _dot(a, b):
    return jnp.dot(a, b, preferred_element_type=F32)


def _dot_nt(a, b):
    return lax.dot_general(a, b, (((1,), (1,)), ((), ())), preferred_element_type=F32)


def _dot_tn(a, b):
    return lax.dot_general(a, b, (((0,), (0,)), ((), ())), preferred_element_type=F32)


def _proj_kernel(x_ref, wq_ref, wk_ref, wv_ref, wgq_ref, wgk_ref, wgv_ref, wgr_ref, wz_ref,
                 wdec_ref, bdec_ref, cos_ref, sa_ref, sb_ref,
                 q_ref, k_ref, vt_ref, gq_ref, gk_ref, gv_ref, rs_ref, gf_ref, gb_ref):
    x = x_ref[...].astype(BF16)
    cos = cos_ref[...]
    sin_a = sa_ref[...]
    sin_b = sb_ref[...]

    def rope(t):
        cols = []
        for j in range(t.shape[1] // LANES):
            blk = t[:, j * LANES:(j + 1) * LANES]
            cols.append(blk * cos + pltpu.roll(blk, LANES - ROT_DIM // 2, 1) * sin_a
                        + pltpu.roll(blk, ROT_DIM // 2, 1) * sin_b)
        return jnp.concatenate(cols, axis=1)

    z = _dot(x, wz_ref[...])
    zz = _dot(z.astype(BF16), wdec_ref[...]) + bdec_ref[...]
    g = -(jnp.maximum(-zz, 0.0) + jnp.log1p(jnp.exp(-jnp.abs(zz)))) / GLA_TAU
    gf_ref[...] = g[:, :GLA_QK]
    gb_ref[...] = g[:, GLA_QK:]
    r = _dot(x, wgr_ref[...])
    rs_ref[...] = (r * jax.nn.sigmoid(r)).astype(BF16)
    q_ref[...] = rope(_dot(x, wq_ref[...])).astype(BF16)
    k_ref[...] = rope(_dot(x, wk_ref[...])).astype(BF16)
    vt_ref[...] = _dot(x, wv_ref[...]).T.astype(BF16)
    gq_ref[...] = _dot(x, wgq_ref[...]).astype(BF16)
    gk_ref[...] = _dot(x, wgk_ref[...]).astype(BF16)
    gv_ref[...] = _dot(x, wgv_ref[...]).astype(BF16)


def _proj_call(x2, B, S, w, tabs):
    M = x2.shape[0]
    tm = PROJ_TM
    ns = S // tm
    full = lambda a: pl.BlockSpec(a.shape, lambda i: (0,) * a.ndim)
    tok = lambda n: pl.BlockSpec((tm, n), lambda i: (i, 0))
    tab = pl.BlockSpec((tm, LANES), lambda i: (i % ns, 0))
    ins = [x2, w["wq"], w["wk"], w["wv"], w["wgq"], w["wgk"], w["wgv"], w["wgr"], w["wz"],
           w["wdec"], w["bdec"], tabs[0], tabs[1], tabs[2]]
    in_specs = [tok(D_MODEL)] + [full(a) for a in ins[1:11]] + [tab, tab, tab]
    out_shape = [
        jax.ShapeDtypeStruct((M, 512), BF16),
        jax.ShapeDtypeStruct((M, 512), BF16),
        jax.ShapeDtypeStruct((B, 512, S), BF16),
        jax.ShapeDtypeStruct((M, GLA_QK), BF16),
        jax.ShapeDtypeStruct((M, GLA_QK), BF16),
        jax.ShapeDtypeStruct((M, GLA_WIDTH), BF16),
        jax.ShapeDtypeStruct((M, GLA_WIDTH), BF16),
        jax.ShapeDtypeStruct((M, GLA_QK), F32),
        jax.ShapeDtypeStruct((M, GLA_QK), F32),
    ]
    out_specs = [tok(512), tok(512),
                 pl.BlockSpec((None, 512, tm), lambda i: (i // ns, 0, i % ns)),
                 tok(GLA_QK), tok(GLA_QK), tok(GLA_WIDTH), tok(GLA_WIDTH), tok(GLA_QK), tok(GLA_QK)]
    return pl.pallas_call(
        _proj_kernel, grid=(M // tm,), in_specs=in_specs, out_specs=out_specs, out_shape=out_shape,
        compiler_params=pltpu.CompilerParams(dimension_semantics=("parallel",),
                                             vmem_limit_bytes=VMEM_LIMIT),
        name="proj")(*ins)


def _attn_kernel(lamv_ref, g_ref, q_ref, k_ref, vt_ref, o_ref, *bufs, seq, lam_init):
    tq, tk, rc = ATT_TQ, ATT_TK, ATT_RC
    ncol = 2 * tq
    nch = tk // rc
    n = seq // tk
    nq = seq // tq
    nb = ATT_NBUF
    s_bufs = bufs[:nb]
    p_bufs = bufs[nb:]

    lv = lamv_ref[...]
    lam = (jnp.exp(jnp.sum(lv[0:1] * lv[1:2], keepdims=True))
           - jnp.exp(jnp.sum(lv[2:3] * lv[3:4], keepdims=True)) + lam_init)
    gain = g_ref[...] * (1.0 - lam_init)

    def make_qq(qi):
        qt = q_ref[pl.ds(pl.multiple_of(qi * tq, tq), tq), :].astype(F32).T
        row = lax.broadcasted_iota(jnp.int32, qt.shape, 0)
        return jnp.concatenate([jnp.where(row < DA_DK, qt, 0.0),
                                jnp.where(row >= DA_DK, qt, 0.0)], axis=1).astype(BF16)

    def qk_stage(qq, t):
        sc = _dot(k_ref[t * tk:(t + 1) * tk, :], qq)
        s_bufs[t % nb][...] = sc
        return jnp.max(sc.reshape(tk // 8, 8, ncol), axis=0)

    def softmax_stage(t, mx, m, l):
        m_new = jnp.maximum(m, jnp.max(mx, axis=0, keepdims=True))
        alpha = jnp.exp2(m - m_new)
        mb = jnp.broadcast_to(m_new, (8, ncol))[None]
        ls = None
        for c in range(nch):
            rows = slice(c * rc, (c + 1) * rc)
            p = jnp.exp2(s_bufs[t % nb][rows, :].reshape(rc // 8, 8, ncol) - mb)
            part = jnp.sum(p, axis=0)
            ls = part if ls is None else ls + part
            p_bufs[t % nb][rows, :] = p.reshape(rc, ncol).astype(BF16)
        return m_new, alpha * l + jnp.sum(ls, axis=0, keepdims=True), alpha

    def pv_stage(t, alpha, acc):
        return alpha * acc + _dot(vt_ref[:, t * tk:(t + 1) * tk], p_bufs[t % nb][...])

    def finalize(qi, l, acc):
        on = acc / l
        o = on[:, :tq] - lam * on[:, tq:]
        ms = jnp.mean(o * o, axis=0, keepdims=True)
        y = o * lax.rsqrt(ms + RMS_EPS) * gain
        o_ref[pl.ds(pl.multiple_of(qi * tq, tq), tq), :] = y.T.astype(BF16)

    m_init = jnp.full((1, ncol), -jnp.inf, F32)
    l_init = jnp.zeros((1, ncol), F32)
    acc_init = jnp.zeros((DA_DV, ncol), F32)

    def query_tile(qi, qq, state, first):
        mx, m, l, alpha, acc = state
        for t in range(n):
            mx_new = qk_stage(qq, t)
            if t == 0 and not first:
                acc = pv_stage(n - 2, alpha, acc)
                m, l, alpha = softmax_stage(n - 1, mx, m, l)
            elif t == 1:
                if not first:
                    acc = pv_stage(n - 1, alpha, acc)
                    finalize(qi - 1, l, acc)
                m, l, alpha = softmax_stage(0, mx, m_init, l_init)
                acc = acc_init
            elif t >= 2:
                acc = pv_stage(t - 2, alpha, acc)
                m, l, alpha = softmax_stage(t - 1, mx, m, l)
            mx = mx_new
        return mx, m, l, alpha, acc

    state = (jnp.zeros((8, ncol), F32), m_init, l_init, jnp.ones((1, ncol), F32), acc_init)
    state = query_tile(0, make_qq(0), state, True)

    def body(qi, carry):
        qq, state = carry
        qq_next = make_qq(jnp.minimum(qi + 1, nq - 1))
        return qq_next, query_tile(qi, qq, state, False)

    _, state = lax.fori_loop(1, nq, body, (make_qq(1), state))
    mx, m, l, alpha, acc = state
    acc = pv_stage(n - 2, alpha, acc)
    m, l, alpha = softmax_stage(n - 1, mx, m, l)
    acc = pv_stage(n - 1, alpha, acc)
    finalize(nq - 1, l, acc)


def _attn_call(q, k, vt, lamv, subg, lam_init):
    B, S, _ = q.shape
    seq_blk = pl.BlockSpec((None, S, LANES), lambda b, h: (b, 0, h))
    return pl.pallas_call(
        functools.partial(_attn_kernel, seq=S, lam_init=lam_init),
        grid=(B, DA_HEADS),
        in_specs=[
            pl.BlockSpec(lamv.shape, lambda b, h: (0, 0)),
            pl.BlockSpec(subg.shape, lambda b, h: (0, 0)),
            seq_blk, seq_blk,
            pl.BlockSpec((None, LANES, S), lambda b, h: (b, h, 0)),
        ],
        out_specs=seq_blk,
        out_shape=jax.ShapeDtypeStruct((B, S, DA_WIDTH), BF16),
        scratch_shapes=[pltpu.VMEM((ATT_TK, 2 * ATT_TQ), F32)] * ATT_NBUF
        + [pltpu.VMEM((ATT_TK, 2 * ATT_TQ), BF16)] * ATT_NBUF,
        compiler_params=pltpu.CompilerParams(
            dimension_semantics=("parallel", "parallel"),
            vmem_limit_bytes=VMEM_LIMIT),
        name="diff_attn")(lamv, subg, q, k, vt)


GLA_MXU_LEVELS = (4, 2, 1)


def _split2(g):
    g1 = g.astype(BF16)
    return g1, (g - g1.astype(F32)).astype(BF16)


def _gla_tables(C):
    t = np.arange(C)
    tri_i = (t[None, :] <= t[:, None]).astype(np.float32)
    tri_s = (t[None, :] < t[:, None]).astype(np.float32)
    blocks = []
    for w in GLA_MXU_LEVELS:
        rho = (t // (2 * w)) * 2 * w + w
        odd = ((t & w) != 0).astype(np.float32)[:, None]
        d_f = tri_i - tri_i[rho]
        d_b = tri_s - tri_s[rho]
        blocks.append(np.concatenate([odd * d_f, (odd - 1.0) * d_b], 1))
        blocks.append(np.concatenate([(odd - 1.0) * d_f, odd * d_b], 1))
    tab = np.concatenate(blocks, 0)
    x = t[:, None] ^ t[None, :]
    hb = np.where(x > 0, 2 ** np.floor(np.log2(np.maximum(x, 1))), 0).astype(np.int32)
    tri = np.concatenate([tri_i, tri_s], 0)
    return (jnp.asarray(np.concatenate([tab, tab], 1), BF16),
            jnp.asarray(np.concatenate([tri, tri], 1), BF16), jnp.asarray(hb))


def _pair_ref_rows(c, w):
    C, n = c.shape
    c3 = c.reshape(C // (2 * w), 2 * w, n)
    return jnp.broadcast_to(c3[:, w:w + 1, :], c3.shape).reshape(C, n)


def _gla_kernel(tab_ref, tri_ref, hb_ref, q_ref, k_ref, v_ref, gf_ref, gb_ref,
                qb_ref, kb_ref, vb_ref, gbb_ref, om_ref, ob_ref, sf_ref, sb_ref):
    @pl.when(pl.program_id(1) == 0)
    def _():
        sf_ref[...] = jnp.zeros_like(sf_ref)
        sb_ref[...] = jnp.zeros_like(sb_ref)

    chunks = []
    for j in range(GLA_SUB):
        rf = pl.ds(j * GLA_C, GLA_C)
        rb = pl.ds((GLA_SUB - 1 - j) * GLA_C, GLA_C)
        chunks.append(_gla_chunk(tab_ref, tri_ref, hb_ref, q_ref.at[rf], k_ref.at[rf], v_ref.at[rf],
                                 gf_ref.at[rf], gb_ref.at[rf], qb_ref.at[rb], kb_ref.at[rb],
                                 vb_ref.at[rb], gbb_ref.at[rb], om_ref.at[rf], ob_ref.at[rb],
                                 sf_ref, sb_ref))
    for _ in range(GLA_PHASES):
        for chunk in chunks:
            next(chunk)


GLA_PHASES = 3


def _gla_chunk(tab_ref, tri_ref, hb_ref, q_ref, k_ref, v_ref, gf_ref, gb_ref,
               qb_ref, kb_ref, vb_ref, gbb_ref, om_ref, ob_ref, sf_ref, sb_ref):
    C = GLA_C
    hb = hb_ref[...]
    rowi = lax.broadcasted_iota(jnp.int32, (C, 1), 0)
    lane = lax.broadcasted_iota(jnp.int32, (C, LANES), 1)
    head_lo = lane < GLA_DK
    sr = lax.broadcasted_iota(jnp.int32, (2 * GLA_DV, 2 * GLA_DK), 0)
    sc = lax.broadcasted_iota(jnp.int32, (2 * GLA_DV, 2 * GLA_DK), 1)
    sdiag = (sr < GLA_DV) == (sc < GLA_DK)

    q2 = qb_ref[...].astype(F32)
    k2 = kb_ref[...].astype(F32)
    v2 = vb_ref[...]
    g2 = gbb_ref[...]
    cx = _dot(tri_ref[C:2 * C, :], jnp.concatenate(_split2(g2), axis=0))
    tot = cx[C - 1:C, :] + g2[C - 1:C, :]
    qe2 = (q2 * jnp.exp(tot - cx)).astype(BF16)
    ke2 = (k2 * jnp.exp(cx)).astype(BF16)
    dec2 = jnp.exp(tot)

    q = q_ref[...].astype(F32)
    k = k_ref[...].astype(F32)
    v = v_ref[...]
    gf1, gf2 = _split2(gf_ref[...])
    gb1, gb2 = _split2(gb_ref[...])
    cf = _dot(tri_ref[0:C, :], jnp.concatenate([gf1, gf2], axis=0))
    cb = _dot(tri_ref[C:2 * C, :], jnp.concatenate([gb1, gb2], axis=0))
    g4 = jnp.concatenate([gf1, gb1, gf2, gb2], axis=0)

    levels = [C >> (i + 1) for i in range(C.bit_length() - 1)]
    operands = []
    for w in levels:
        if w in GLA_MXU_LEVELS:
            base = 2 * C * GLA_MXU_LEVELS.index(w)
            arg = _dot(tab_ref[base:base + 2 * C, :], g4)
            arg_q, arg_k = arg[:C], arg[C:]
        else:
            odd = (rowi & w) != 0
            d_f = cf - _pair_ref_rows(cf, w)
            d_b = cb - _pair_ref_rows(cb, w)
            arg_q = jnp.where(odd, d_f, -d_b)
            arg_k = jnp.where(odd, d_b, -d_f)
        operands.append(((q * jnp.exp(arg_q)).astype(BF16), (k * jnp.exp(arg_k)).astype(BF16)))
    operands.append((q_ref[...], k_ref[...]))
    cf_last = cf[C - 1:C, :]
    qe = (q * jnp.exp(cf)).astype(BF16)
    ke = (k * jnp.exp(cf_last - cf)).astype(BF16)
    dec = jnp.exp(cf_last)
    yield

    def head_blocks(qt, kt):
        out = []
        for h in range(GLA_HEADS):
            pr = slice((h // 2) * LANES, (h // 2 + 1) * LANES)
            keep = head_lo if h % 2 == 0 else jnp.logical_not(head_lo)
            out.append(_dot_nt(jnp.where(keep, qt[:, pr], jnp.zeros((), BF16)), kt[:, pr]))
        return out

    blocks = [head_blocks(qt, kt) for qt, kt in operands]
    yield

    a = [2.0 * blocks[-1][h] for h in range(GLA_HEADS)]
    for w, blk in zip(levels, blocks[:-1]):
        sel = hb == w
        a = [jnp.where(sel, blk[h], a[h]) for h in range(GLA_HEADS)]

    outs = []
    for p in range(2):
        pr = slice(p * LANES, (p + 1) * LANES)
        vr = slice(p * 2 * GLA_DV, (p + 1) * 2 * GLA_DV)
        st = sb_ref[p]
        outs.append(_dot_nt(qe2[:, pr], st.astype(BF16)))
        upd = _dot_tn(v2[:, vr], ke2[:, pr])
        sb_ref[p] = dec2[:, pr] * st + jnp.where(sdiag, upd, 0.0)
    ob_ref[...] = jnp.concatenate(outs, axis=1)

    outs = []
    for p in range(2):
        pr = slice(p * LANES, (p + 1) * LANES)
        vr = slice(p * 2 * GLA_DV, (p + 1) * 2 * GLA_DV)
        st = sf_ref[p]
        inter = _dot_nt(qe[:, pr], st.astype(BF16))
        for hh in range(2):
            h = 2 * p + hh
            vh = slice(h * GLA_DV, (h + 1) * GLA_DV)
            outs.append(_dot(a[h].astype(BF16), v[:, vh]) + inter[:, hh * GLA_DV:(hh + 1) * GLA_DV])
        upd = _dot_tn(v[:, vr], ke[:, pr])
        sf_ref[p] = dec[:, pr] * st + jnp.where(sdiag, upd, 0.0)
    om_ref[...] = jnp.concatenate(outs, axis=1)
    yield


def _gla_call(gq, gk, gv, gf, gb, B, S):
    M = gq.shape[0]
    rows = GLA_SUB * GLA_C
    n = S // rows
    fw = lambda width: pl.BlockSpec((rows, width), lambda b, i: (b * n + i, 0))
    bw = lambda width: pl.BlockSpec((rows, width), lambda b, i: (b * n + n - 1 - i, 0))
    tables = _gla_tables(GLA_C)
    return pl.pallas_call(
        _gla_kernel, grid=(B, n),
        in_specs=[pl.BlockSpec(t.shape, lambda b, i: (0, 0)) for t in tables]
        + [fw(GLA_QK), fw(GLA_QK), fw(GLA_WIDTH), fw(GLA_QK), fw(GLA_QK),
           bw(GLA_QK), bw(GLA_QK), bw(GLA_WIDTH), bw(GLA_QK)],
        out_specs=[fw(GLA_WIDTH), bw(GLA_WIDTH)],
        out_shape=[jax.ShapeDtypeStruct((M, GLA_WIDTH), F32)] * 2,
        scratch_shapes=[pltpu.VMEM((2, 2 * GLA_DV, 2 * GLA_DK), F32)] * 2,
        compiler_params=pltpu.CompilerParams(dimension_semantics=("parallel", "arbitrary"),
                                             vmem_limit_bytes=VMEM_LIMIT),
        name="gla")(*tables, gq, gk, gv, gf, gb, gq, gk, gv, gb)


def _layer_norm(h, g, b):
    mu = jnp.mean(h, axis=-1, keepdims=True)
    d = h - mu
    var = jnp.mean(d * d, axis=-1, keepdims=True)
    return d * lax.rsqrt(var + LN_EPS) * g + b


def _lockstep(parts):
    parts = list(parts)
    while parts:
        parts = [p for p in parts if next(p, StopIteration) is not StopIteration]


def _row_groups(ref, count):
    rows = ref.shape[0] // count
    return [ref.at[pl.ds(i * rows, rows)] for i in range(count)]


def _merge_rows(x_ref, ya_ref, om_ref, ob_ref, rs_ref, wg_ref, bg_ref, gn_ref,
                wa_ref, wb_ref, wo_ref, lg_ref, lb_ref, o_ref, alpha):
    x = x_ref[...]
    gate_logits = _dot(x.astype(BF16), wg_ref[...]) + bg_ref[...]
    pa = _dot(ya_ref[...], wa_ref[...])
    o = om_ref[...] + ob_ref[...]
    gn = gn_ref[...]
    cols = []
    for h in range(GLA_HEADS):
        oh = o[:, h * GLA_DV:(h + 1) * GLA_DV]
        ms = jnp.mean(oh * oh, axis=-1, keepdims=True)
        cols.append(oh * lax.rsqrt(ms + RMS_EPS) * gn)
    yb = (jnp.concatenate(cols, axis=1) * rs_ref[...].astype(F32)).astype(BF16)
    yield
    pb = _dot(yb, wb_ref[...])
    gates = jax.nn.sigmoid(gate_logits)
    mixin = (gates[:, :D_MODEL] * pa + gates[:, D_MODEL:] * pb).astype(BF16)
    yield
    mix = _dot(mixin, wo_ref[...])
    yield
    o_ref[...] = _layer_norm(alpha * x + mix, lg_ref[...], lb_ref[...])


def _merge_kernel(x_ref, ya_ref, om_ref, ob_ref, rs_ref, wg_ref, bg_ref, gn_ref,
                  wa_ref, wb_ref, wo_ref, lg_ref, lb_ref, o_ref, *, alpha):
    tiles = [_row_groups(r, ROW_GROUPS) for r in (x_ref, ya_ref, om_ref, ob_ref, rs_ref, o_ref)]
    _lockstep(_merge_rows(x, ya, om, ob, rs, wg_ref, bg_ref, gn_ref, wa_ref, wb_ref, wo_ref,
                          lg_ref, lb_ref, o, alpha) for x, ya, om, ob, rs, o in zip(*tiles))


def _merge_call(x2, ya, om, ob, rs, w, alpha):
    M = x2.shape[0]
    tm = MERGE_TM
    full = lambda a: pl.BlockSpec(a.shape, lambda i: (0,) * a.ndim)
    tok = lambda n: pl.BlockSpec((tm, n), lambda i: (i, 0))
    ws = [w["wgate"], w["bgate"], w["gnorm"], w["wbra"], w["wbrb"], w["wout"], w["ln1g"], w["ln1b"]]
    return pl.pallas_call(
        functools.partial(_merge_kernel, alpha=alpha), grid=(M // tm,),
        in_specs=[tok(D_MODEL), tok(512), tok(512), tok(512), tok(512)] + [full(a) for a in ws],
        out_specs=tok(D_MODEL), out_shape=jax.ShapeDtypeStruct((M, D_MODEL), F32),
        compiler_params=pltpu.CompilerParams(dimension_semantics=("parallel",),
                                             vmem_limit_bytes=VMEM_LIMIT),
        name="merge")(x2, ya, om, ob, rs, *ws)


def _mlp_rows(x_ref, w1_ref, w2_ref, lg_ref, lb_ref, o_ref, alpha):
    x = x_ref[...]
    h = jnp.maximum(_dot(x.astype(BF16), w1_ref[...]), 0.0)
    yield
    h = (h * h).astype(BF16)
    yield
    h = _dot(h, w2_ref[...])
    yield
    o_ref[...] = _layer_norm(alpha * x + h, lg_ref[...], lb_ref[...])


def _mlp_kernel(x_ref, w1_ref, w2_ref, lg_ref, lb_ref, o_ref, *, alpha):
    _lockstep(_mlp_rows(x, w1_ref, w2_ref, lg_ref, lb_ref, o, alpha)
              for x, o in zip(_row_groups(x_ref, ROW_GROUPS), _row_groups(o_ref, ROW_GROUPS)))


def _mlp_call(x1, w, alpha):
    M = x1.shape[0]
    tm = MLP_TM
    once = lambda a: pl.BlockSpec(a.shape, lambda i: (0,) * a.ndim, pipeline_mode=pl.Buffered(1))
    tok = pl.BlockSpec((tm, D_MODEL), lambda i: (i, 0))
    ws = [w["w1"], w["w2"], w["ln2g"], w["ln2b"]]
    return pl.pallas_call(
        functools.partial(_mlp_kernel, alpha=alpha), grid=(M // tm,),
        in_specs=[tok] + [once(a) for a in ws],
        out_specs=tok, out_shape=jax.ShapeDtypeStruct((M, D_MODEL), F32),
        compiler_params=pltpu.CompilerParams(dimension_semantics=("parallel",),
                                             vmem_limit_bytes=VMEM_LIMIT),
        name="mlp")(x1, *ws)


def _rope_tables(S):
    half = ROT_DIM // 2
    inv = 1.0 / (ROPE_THETA ** (jnp.arange(0, ROT_DIM, 2, dtype=F32) / ROT_DIM))
    ang = jnp.arange(S, dtype=F32)[:, None] * inv[None, :]
    cos, sin = jnp.cos(ang), jnp.sin(ang)
    d = jnp.arange(LANES) % DA_DK
    idx = d % half
    cos_t = jnp.where(d[None, :] < ROT_DIM, cos[:, idx], 1.0)
    sin_a = jnp.where(d[None, :] < half, -sin[:, idx], 0.0)
    sin_b = jnp.where((d[None, :] >= half) & (d[None, :] < ROT_DIM), sin[:, idx], 0.0)
    return cos_t.astype(F32), sin_a.astype(F32), sin_b.astype(F32)


def _layer_weights(l, w_in, subln_g, w_dec_f, b_dec_f, w_dec_b, b_dec_b, gla_norm_g, w_br_a, w_br_b,
                   b_gate, w_out, ln1_g, ln1_b, w_mlp1, w_mlp2, ln2_g, ln2_b):
    offs = [0]
    for s in IN_SIZES:
        offs.append(offs[-1] + s)
    col = lambda i: w_in[l][:, offs[i]:offs[i + 1]]
    zeros = jnp.zeros((GLA_RANK, GLA_QK), F32)
    row = lambda a: a[l].reshape(1, -1).astype(F32)
    return {
        "wq": (col(0) * (DA_DK ** -0.5 * LOG2E)).astype(BF16), "wk": col(1).astype(BF16), "wv": col(2).astype(BF16),
        "wgq": (col(3) * GLA_DK ** -0.5).astype(BF16), "wgk": col(4).astype(BF16),
        "wgv": col(5).astype(BF16), "wgr": col(6).astype(BF16),
        "wz": jnp.concatenate([col(7), col(8)], axis=1).astype(BF16),
        "wdec": jnp.concatenate([jnp.concatenate([w_dec_f[l], zeros], axis=1),
                                 jnp.concatenate([zeros, w_dec_b[l]], axis=1)], axis=0).astype(BF16),
        "bdec": jnp.concatenate([b_dec_f[l], b_dec_b[l]]).reshape(1, -1).astype(F32),
        "wgate": col(9).astype(BF16), "bgate": row(b_gate),
        "subg": subln_g[l].reshape(-1, 1).astype(F32), "gnorm": row(gla_norm_g),
        "wbra": w_br_a[l].astype(BF16), "wbrb": w_br_b[l].astype(BF16), "wout": w_out[l].astype(BF16),
        "ln1g": row(ln1_g), "ln1b": row(ln1_b),
        "w1": w_mlp1[l].astype(BF16), "w2": w_mlp2[l].astype(BF16),
        "ln2g": row(ln2_g), "ln2b": row(ln2_b),
    }


def kernel(x_prompt, x_sample, w_in, lam_q1, lam_k1, lam_q2, lam_k2, subln_g, w_dec_f, b_dec_f, w_dec_b, b_dec_b, gla_norm_g, w_br_a, w_br_b, b_gate, w_out, ln1_g, ln1_b, w_mlp1, w_mlp2, ln2_g, ln2_b):
    depth = w_in.shape[0]
    alpha = (2 * depth) ** 0.25
    layers = []
    for l in range(depth):
        w = _layer_weights(l, w_in, subln_g, w_dec_f, b_dec_f, w_dec_b, b_dec_b, gla_norm_g,
                           w_br_a, w_br_b, b_gate, w_out, ln1_g, ln1_b, w_mlp1, w_mlp2, ln2_g, ln2_b)
        lamv = jnp.stack([lam_q1[l], lam_k1[l], lam_q2[l], lam_k2[l]]).astype(F32)
        layers.append((w, lamv, 0.8 - 0.6 * math.exp(-0.3 * l)))

    def trunk(x):
        B, S, D = x.shape
        assert D == D_MODEL and S % (ATT_NBUF * ATT_TK) == 0 and S % ATT_TQ == 0, (S, D)
        assert S % PROJ_TM == 0 and S % (GLA_SUB * GLA_C) == 0, S
        assert (B * S) % MERGE_TM == 0 and (B * S) % MLP_TM == 0, (B, S)
        tabs = _rope_tables(S)
        x2 = x.reshape(B * S, D)
        for w, lamv, lam_init in layers:
            q, k, vt, gq, gk, gv, rs, gf, gb = _proj_call(x2, B, S, w, tabs)
            ya = _attn_call(q.reshape(B, S, 512), k.reshape(B, S, 512), vt, lamv, w["subg"], lam_init)
            om, ob = _gla_call(gq, gk, gv, gf, gb, B, S)
            x1 = _merge_call(x2, ya.reshape(B * S, 512), om, ob, rs, w, alpha)
            x2 = _mlp_call(x1, w, alpha)
        return x2.reshape(B, S, D)

    return (trunk(x_prompt), trunk(x_sample))
```
